```python
import jax, jax.numpy as jnp
from jax import lax
import numpy as np

D_MODEL = 2048
BATCH = 4
SEQ = 8192
DEPTH = 1

A_HEADS = 8
A_HEAD_DIM = 128
A_WIDTH = A_HEADS * A_HEAD_DIM
MOBA_BLOCK = 256
MOBA_TOPK = 3
MOBA_Q_CHUNK = 16
B_HEADS = 8
B_KEY_DIM = 128
B_VAL_DIM = 128
B_FWIDTH = B_HEADS * B_KEY_DIM
B_VWIDTH = B_HEADS * B_VAL_DIM
HGRN_CHUNK = 64
FFN_HIDDEN = -(-8 * D_MODEL // (3 * 256)) * 256
IN_SPLIT = (A_WIDTH, A_WIDTH, A_WIDTH, B_FWIDTH, B_FWIDTH, B_VWIDTH, B_VWIDTH, D_MODEL, D_MODEL)
IN_WIDTH = sum(IN_SPLIT)
LN_EPS = 1e-5
RMS_EPS = 1e-6
DEEPNORM_ALPHA = (2.0 * DEPTH) ** 0.25
DEEPNORM_BETA = (8.0 * DEPTH) ** -0.25

kernel_name = 'moba_hgrn2_gated_hybrid_deepnorm'


def layer_norm(x, g, b):
    xf = x.astype(jnp.float32)
    mu = jnp.mean(xf, axis=-1, keepdims=True)
    var = jnp.mean(jnp.square(xf - mu), axis=-1, keepdims=True)
    return ((xf - mu) * lax.rsqrt(var + LN_EPS) * g + b).astype(x.dtype)


def to_heads(t, n_heads):
    Bsz, T, W = t.shape
    return t.reshape(Bsz, T, n_heads, W // n_heads).transpose(0, 2, 1, 3)


def from_heads(t):
    Bsz, H, T, Dh = t.shape
    return t.transpose(0, 2, 1, 3).reshape(Bsz, T, H * Dh)


def moba_attention(q, k, v):
    Bsz, H, T, Dh = q.shape
    T_pad = -(-T // MOBA_BLOCK) * MOBA_BLOCK
    pad = ((0, 0), (0, 0), (0, T_pad - T), (0, 0))
    q = jnp.pad(q, pad)
    k = jnp.pad(k, pad)
    v = jnp.pad(v, pad)
    n_blk = T_pad // MOBA_BLOCK
    n_sel = min(MOBA_TOPK, n_blk)
    scale = Dh ** -0.5
    kb = k.reshape(Bsz, H, n_blk, MOBA_BLOCK, Dh)
    vb = v.reshape(Bsz, H, n_blk, MOBA_BLOCK, Dh)
    k_mean = jnp.mean(kb.astype(jnp.float32), axis=3)
    gate = jnp.einsum('bhtd,bhnd->bhtn', q.astype(jnp.float32), k_mean)
    q_blk = jnp.arange(T_pad) // MOBA_BLOCK
    fully_past = jnp.arange(n_blk)[None, :] < q_blk[:, None]
    gate = jnp.where(fully_past, gate, -jnp.inf)
    _, sel = lax.top_k(gate, n_sel)

    n_q = T_pad // MOBA_Q_CHUNK
    q_c = q.reshape(Bsz, H, n_q, MOBA_Q_CHUNK, Dh).transpose(2, 0, 1, 3, 4)
    sel_c = sel.reshape(Bsz, H, n_q, MOBA_Q_CHUNK, n_sel).transpose(2, 0, 1, 3, 4)
    starts = jnp.arange(n_q, dtype=jnp.int32) * MOBA_Q_CHUNK
    bi = jnp.arange(Bsz)[:, None, None, None]
    hi = jnp.arange(H)[None, :, None, None]

    def chunk_attn(args):
        qc, selc, start = args
        blk = start // MOBA_BLOCK
        blk_start = blk * MOBA_BLOCK
        k_sel = kb[bi, hi, selc]
        v_sel = vb[bi, hi, selc]
        s_sel = jnp.einsum('bhqd,bhqnjd->bhqnj', qc, k_sel).astype(jnp.float32) * scale
        slot_ok = jnp.arange(n_sel) < blk
        s_sel = jnp.where(slot_ok[:, None], s_sel, -jnp.inf)
        k_own = lax.dynamic_slice_in_dim(k, blk_start, MOBA_BLOCK, axis=2)
        v_own = lax.dynamic_slice_in_dim(v, blk_start, MOBA_BLOCK, axis=2)
        s_own = jnp.einsum('bhqd,bhjd->bhqj', qc, k_own).astype(jnp.float32) * scale
        q_pos = start + jnp.arange(MOBA_Q_CHUNK)
        k_pos = blk_start + jnp.arange(MOBA_BLOCK)
        s_own = jnp.where(k_pos[None, :] <= q_pos[:, None], s_own, -jnp.inf)
        scores = jnp.concatenate(
            [s_sel.reshape(Bsz, H, MOBA_Q_CHUNK, n_sel * MOBA_BLOCK), s_own], axis=-1)
        p = jax.nn.softmax(scores, axis=-1).astype(v.dtype)
        p_sel = p[..., :n_sel * MOBA_BLOCK].reshape(Bsz, H, MOBA_Q_CHUNK, n_sel, MOBA_BLOCK)
        p_own = p[..., n_sel * MOBA_BLOCK:]
        return (jnp.einsum('bhqnj,bhqnjd->bhqd', p_sel, v_sel)
                + jnp.einsum('bhqj,bhjd->bhqd', p_own, v_own))

    out = lax.map(chunk_attn, (q_c, sel_c, starts))
    out = out.transpose(1, 2, 0, 3, 4).reshape(Bsz, H, T_pad, Dh)
    return out[:, :, :T]


def hgrn2_recurrence(q, f_logit, i, lower_bound):
    Bsz, H, T, K = q.shape
    V = i.shape[-1]
    C = HGRN_CHUNK
    N = T // C
    lb = lower_bound[None, :, None, :]
    f = lb + (1.0 - lb) * jax.nn.sigmoid(f_logit.astype(jnp.float32))
    log_f = jnp.log(f)
    key = 1.0 - f

    def chunks(t):
        return t.astype(jnp.float32).reshape(Bsz, H, N, C, t.shape[-1]).transpose(2, 0, 1, 3, 4)

    causal = jnp.arange(C)[:, None] >= jnp.arange(C)[None, :]

    def step(S, inp):
        qc, kc, vc, gc = inp
        b = jnp.cumsum(gc, axis=2)
        o_inter = jnp.einsum('bhtk,bhkv->bhtv', qc * jnp.exp(b), S)
        rel = jnp.where(causal[None, None, :, :, None],
                        b[:, :, :, None, :] - b[:, :, None, :, :], -jnp.inf)
        A = jnp.einsum('bhtk,bhsk,bhtsk->bhts', qc, kc, jnp.exp(rel))
        o_intra = jnp.einsum('bhts,bhsv->bhtv', A, vc)
        b_last = b[:, :, -1:, :]
        S_new = (jnp.exp(b_last[:, :, 0, :])[..., None] * S
                 + jnp.einsum('bhsk,bhsv->bhkv', kc * jnp.exp(b_last - b), vc))
        return S_new, o_inter + o_intra

    S0 = jnp.zeros((Bsz, H, K, V), jnp.float32)
    _, outs = lax.scan(step, S0, (chunks(q), chunks(key), chunks(i), chunks(log_f)))
    o = outs.transpose(1, 2, 0, 3, 4).reshape(Bsz, H, T, V)
    return o.astype(i.dtype)


def token_mixer(h, w_in, w_proj_a, w_proj_b, w_out, hgrn_norm_g, lower_bound):
    z = h @ w_in
    parts = []
    off = 0
    for n in IN_SPLIT:
        parts.append(z[..., off:off + n])
        off += n
    qa, ka, va, qb, fb, ib, ogb, ga, gb = parts
    o_a = moba_attention(to_heads(qa, A_HEADS), to_heads(ka, A_HEADS), to_heads(va, A_HEADS))
    o_a = from_heads(o_a)
    o_b = hgrn2_recurrence(to_heads(jax.nn.silu(qb), B_HEADS), to_heads(fb, B_HEADS),
                           to_heads(ib, B_HEADS), lower_bound.reshape(B_HEADS, B_KEY_DIM))
    o_bf = o_b.astype(jnp.float32)
    o_bf = o_bf * lax.rsqrt(jnp.mean(jnp.square(o_bf), axis=-1, keepdims=True) + RMS_EPS)
    o_bf = o_bf * hgrn_norm_g.reshape(B_HEADS, 1, B_VAL_DIM).astype(jnp.float32)
    o_b = from_heads(o_bf.astype(h.dtype)) * jax.nn.sigmoid(ogb)
    y_a = o_a @ w_proj_a
    y_b = o_b @ w_proj_b
    merged = jax.nn.sigmoid(ga) * y_a + jax.nn.sigmoid(gb) * y_b
    return merged @ w_out


def swiglu_ffn(h, w_gate, w_up, w_down):
    return (jax.nn.silu(h @ w_gate) * (h @ w_up)) @ w_down


def setup_inputs(seed: int = 0) -> dict:
    key = jax.random.key(seed)
    ks = jax.random.split(key, 16)

    def nrm(k, shape, scale):
        return jax.random.normal(k, shape, jnp.float32) * scale

    return {
        'x': nrm(ks[0], (BATCH, SEQ, D_MODEL), 1.0),
        'w_in': nrm(ks[1], (DEPTH, D_MODEL, IN_WIDTH), D_MODEL ** -0.5),
        'w_proj_a': nrm(ks[2], (DEPTH, A_WIDTH, D_MODEL), A_WIDTH ** -0.5 * DEEPNORM_BETA),
        'w_proj_b': nrm(ks[3], (DEPTH, B_VWIDTH, D_MODEL), B_VWIDTH ** -0.5 * DEEPNORM_BETA),
        'w_out': nrm(ks[4], (DEPTH, D_MODEL, D_MODEL), D_MODEL ** -0.5 * DEEPNORM_BETA),
        'hgrn_norm_g': 1.0 + nrm(ks[5], (DEPTH, B_VWIDTH), 0.02),
        'hgrn_lb_logits': nrm(ks[6], (DEPTH + 1, B_FWIDTH), 0.5),
        'ln1_g': 1.0 + nrm(ks[7], (DEPTH, D_MODEL), 0.02),
        'ln1_b': nrm(ks[8], (DEPTH, D_MODEL), 0.02),
        'w_gate_ffn': nrm(ks[9], (DEPTH, D_MODEL, FFN_HIDDEN), D_MODEL ** -0.5),
        'w_up_ffn': nrm(ks[10], (DEPTH, D_MODEL, FFN_HIDDEN), D_MODEL ** -0.5),
        'w_down_ffn': nrm(ks[11], (DEPTH, FFN_HIDDEN, D_MODEL), FFN_HIDDEN ** -0.5 * DEEPNORM_BETA),
        'ln2_g': 1.0 + nrm(ks[12], (DEPTH, D_MODEL), 0.02),
        'ln2_b': nrm(ks[13], (DEPTH, D_MODEL), 0.02),
    }


def reference(x, w_in, w_proj_a, w_proj_b, w_out, hgrn_norm_g, hgrn_lb_logits,
              ln1_g, ln1_b, w_gate_ffn, w_up_ffn, w_down_ffn, ln2_g, ln2_b):
    lb_all = jnp.cumsum(jax.nn.softmax(hgrn_lb_logits.astype(jnp.float32), axis=0), axis=0)
    h = x
    for l in range(DEPTH):
        mix = token_mixer(h, w_in[l], w_proj_a[l], w_proj_b[l], w_out[l], hgrn_norm_g[l], lb_all[l])
        h = layer_norm(DEEPNORM_ALPHA * h + mix, ln1_g[l], ln1_b[l])
        ff = swiglu_ffn(h, w_gate_ffn[l], w_up_ffn[l], w_down_ffn[l])
        h = layer_norm(DEEPNORM_ALPHA * h + ff, ln2_g[l], ln2_b[l])
    return h
```

```python
import functools
import math

import jax
import jax.numpy as jnp
from jax import lax
from jax.experimental import pallas as pl
from jax.experimental.pallas import tpu as pltpu

F32 = jnp.float32
BF16 = jnp.bfloat16

HEAD_DIM = 128
MOBA_BLOCK = 256
MOBA_TOPK = 3
LN_EPS = 1e-5
RMS_EPS = 1e-6
NEG_INF = float("-inf")

HGRN_CHUNK = 64
HGRN_SUB = 16
HGRN_SAFE_CHUNK = 8
HGRN_MAX_EXPONENT = 60.0

VMEM_LIMIT_BYTES = 56 * 1024 * 1024

NT_DIMS = (((1,), (1,)), ((), ()))
TN_DIMS = (((0,), (0,)), ((), ()))


def _tile(n, pref):
    if n <= pref:
        return n
    t = pref - pref % 128
    while t >= 128:
        if n % t == 0:
            return t
        t -= 128
    raise ValueError(f"no 128-aligned tile for {n}")


def _params(*sem):
    return pltpu.CompilerParams(dimension_semantics=sem, vmem_limit_bytes=VMEM_LIMIT_BYTES)


def _dot(a, b):
    return jnp.dot(a, b, preferred_element_type=F32)


def _layer_norm(pre, g, b):
    mu = jnp.mean(pre, axis=-1, keepdims=True)
    d = pre - mu
    var = jnp.mean(d * d, axis=-1, keepdims=True)
    return d * lax.rsqrt(var + LN_EPS) * g + b


def _inproj_kernel(x_ref, w_ref, o_ref, xb_ref):
    @pl.when(pl.program_id(1) == 0)
    def _():
        xb_ref[...] = x_ref[...].astype(BF16)

    o_ref[...] = _dot(xb_ref[...], w_ref[...]).astype(o_ref.dtype)


def _inproj(x, w):
    m, k = x.shape
    n = w.shape[1]
    tm, tn = _tile(m, 1024), _tile(n, 1024)
    return pl.pallas_call(
        _inproj_kernel,
        grid=(m // tm, n // tn),
        in_specs=[pl.BlockSpec((tm, k), lambda i, j: (i, 0)),
                  pl.BlockSpec((k, tn), lambda i, j: (0, j))],
        out_specs=pl.BlockSpec((tm, tn), lambda i, j: (i, j)),
        out_shape=jax.ShapeDtypeStruct((m, n), BF16),
        scratch_shapes=[pltpu.VMEM((tm, k), BF16)],
        compiler_params=_params("parallel", "arbitrary"),
        name="inproj",
    )(x, w)


def _moba_kernel(q_ref, k_ref, v_ref, o_ref, kmean_ref, vt_ref, bias_ref, m_ref, l_ref, acc_ref,
                 *, nb, nbp, scale):
    i = pl.program_id(2)
    blk = MOBA_BLOCK

    @pl.when(i == 0)
    def _():
        kmean_ref[...] = jnp.zeros_like(kmean_ref)

        def prep(n, carry):
            off = pl.multiple_of(n * blk, blk)
            kf = k_ref[pl.ds(off, blk), :].astype(F32)
            kmean_ref[pl.ds(n, 1), :] = jnp.sum(kf, axis=0, keepdims=True) * (1.0 / blk)
            vt_ref[n] = v_ref[pl.ds(off, blk), :].astype(F32).T.astype(BF16)
            return carry

        lax.fori_loop(0, nb, prep, 0)

    q = q_ref[...]
    qt = q.astype(F32).T.astype(BF16)

    km = kmean_ref[...]
    km_hi = km.astype(BF16)
    km_lo = (km - km_hi.astype(F32)).astype(BF16)
    gate = _dot(km_hi, qt) + _dot(km_lo, qt)
    row = lax.broadcasted_iota(jnp.int32, (nbp, blk), 0)
    g = jnp.where(row < i, gate, NEG_INF)
    sel = jnp.zeros((nbp, blk), dtype=jnp.bool_)
    for _ in range(MOBA_TOPK):
        mx = jnp.max(g, axis=0, keepdims=True)
        idx = jnp.min(jnp.where(g == mx, row, nbp), axis=0, keepdims=True)
        pick = (row == idx) & (mx > NEG_INF)
        sel = sel | pick
        g = jnp.where(pick, NEG_INF, g)
    bias_ref[...] = jnp.where(sel, 0.0, NEG_INF)

    own = pl.multiple_of(i * blk, blk)
    s = _dot(k_ref[pl.ds(own, blk), :], qt) * scale
    kr = lax.broadcasted_iota(jnp.int32, (blk, blk), 0)
    qc = lax.broadcasted_iota(jnp.int32, (blk, blk), 1)
    s = jnp.where(kr <= qc, s, NEG_INF)
    m0 = jnp.max(s, axis=0, keepdims=True)
    p = jnp.exp(s - m0)
    m_ref[...] = m0
    l_ref[...] = jnp.sum(p, axis=0, keepdims=True)
    acc_ref[...] = _dot(vt_ref[i], p.astype(BF16))

    def body(j, carry):
        off = pl.multiple_of(j * blk, blk)
        sj = _dot(k_ref[pl.ds(off, blk), :], qt) * scale + bias_ref[pl.ds(j, 1), :]
        m_old = m_ref[...]
        m_new = jnp.maximum(m_old, jnp.max(sj, axis=0, keepdims=True))
        alpha = jnp.exp(m_old - m_new)
        pj = jnp.exp(sj - m_new)
        l_ref[...] = alpha * l_ref[...] + jnp.sum(pj, axis=0, keepdims=True)
        acc_ref[...] = alpha * acc_ref[...] + _dot(vt_ref[j], pj.astype(BF16))
        m_ref[...] = m_new
        return carry

    lax.fori_loop(0, i, body, 0)

    out = acc_ref[...] / l_ref[...]
    o_ref[...] = out.T.astype(o_ref.dtype)


def _moba(z, bsz, seq, heads, q_col, k_col, v_col):
    blk = MOBA_BLOCK
    assert seq % blk == 0
    nb = seq // blk
    nbp = -(-nb // 8) * 8
    z3 = z.reshape(bsz, seq, z.shape[1])
    kern = functools.partial(_moba_kernel, nb=nb, nbp=nbp, scale=HEAD_DIM ** -0.5)
    return pl.pallas_call(
        kern,
        grid=(bsz, heads, nb),
        in_specs=[pl.BlockSpec((blk, HEAD_DIM), lambda b, h, i: (b * nb + i, q_col + h)),
                  pl.BlockSpec((None, seq, HEAD_DIM), lambda b, h, i: (b, 0, k_col + h)),
                  pl.BlockSpec((None, seq, HEAD_DIM), lambda b, h, i: (b, 0, v_col + h))],
        out_specs=pl.BlockSpec((blk, HEAD_DIM), lambda b, h, i: (b * nb + i, h)),
        out_shape=jax.ShapeDtypeStruct((bsz * seq, heads * HEAD_DIM), BF16),
        scratch_shapes=[pltpu.VMEM((nbp, HEAD_DIM), F32),
                        pltpu.VMEM((nb, HEAD_DIM, blk), BF16),
                        pltpu.VMEM((nbp, blk), F32),
                        pltpu.VMEM((1, blk), F32),
                        pltpu.VMEM((1, blk), F32),
                        pltpu.VMEM((HEAD_DIM, blk), F32)],
        compiler_params=_params("parallel", "parallel", "arbitrary"),
        name="moba",
    )(z, z3, z3)


def _hgrn_kernel(q_ref, f_ref, i_ref, og_ref, lb_ref, gain_ref, o_ref, st_ref, *, tt, chunk, sub, unroll):
    @pl.when(pl.program_id(2) == 0)
    def _():
        st_ref[...] = jnp.zeros_like(st_ref)

    n_blocks = chunk // sub
    stack = n_blocks * chunk
    lb = lb_ref[...]
    gain = gain_ref[...]
    tri = (lax.broadcasted_iota(jnp.int32, (chunk, chunk), 0)
           >= lax.broadcasted_iota(jnp.int32, (chunk, chunk), 1)).astype(BF16)
    r_i = lax.broadcasted_iota(jnp.int32, (chunk, stack), 0)
    c_i = lax.broadcasted_iota(jnp.int32, (chunk, stack), 1)
    keep = ((c_i // chunk) == (r_i // sub)) & ((c_i % chunk) <= r_i)
    row_c = lax.broadcasted_iota(jnp.int32, (chunk, HEAD_DIM), 0)

    def one_chunk(n, carry):
        off = pl.multiple_of(n * chunk, chunk)
        xq = q_ref[pl.ds(off, chunk), :].astype(F32)
        qs = xq * jax.nn.sigmoid(xq)
        f = lb + (1.0 - lb) * jax.nn.sigmoid(f_ref[pl.ds(off, chunk), :].astype(F32))
        logf = jnp.log(f)
        kk = 1.0 - f
        vb = i_ref[pl.ds(off, chunk), :]

        hi = logf.astype(BF16)
        r1 = logf - hi.astype(F32)
        mid = r1.astype(BF16)
        lo = (r1 - mid.astype(F32)).astype(BF16)
        b = _dot(tri, hi) + _dot(tri, mid) + _dot(tri, lo)
        b_last = b[chunk - 1:chunk, :]

        st = st_ref[...]
        qe = (qs * jnp.exp(b)).astype(BF16)
        o_inter = lax.dot_general(qe, st.astype(BF16), NT_DIMS, preferred_element_type=F32)

        ref_rows = jnp.concatenate(
            [jnp.broadcast_to(b[blk * sub:blk * sub + 1, :], (sub, HEAD_DIM)) for blk in range(n_blocks)],
            axis=0)
        qt = (qs * jnp.exp(b - ref_rows)).astype(BF16)
        kh = jnp.concatenate(
            [(kk * jnp.exp(jnp.where(row_c < (blk + 1) * sub, b[blk * sub:blk * sub + 1, :] - b, NEG_INF))
              ).astype(BF16) for blk in range(n_blocks)], axis=0)
        res = lax.dot_general(qt, kh, NT_DIMS, preferred_element_type=F32)
        a = jnp.where(keep, res, 0.0).astype(BF16)
        o = o_inter + _dot(a, jnp.concatenate([vb] * n_blocks, axis=0))

        kd = (kk * jnp.exp(b_last - b)).astype(BF16)
        st_ref[...] = st * jnp.exp(b_last) + lax.dot_general(vb, kd, TN_DIMS, preferred_element_type=F32)

        ms = jnp.mean(o * o, axis=-1, keepdims=True)
        on = o * lax.rsqrt(ms + RMS_EPS) * gain
        og = og_ref[pl.ds(off, chunk), :].astype(F32)
        o_ref[pl.ds(off, chunk), :] = (on * jax.nn.sigmoid(og)).astype(o_ref.dtype)
        return carry

    lax.fori_loop(0, tt // chunk, one_chunk, 0, unroll=unroll)


def _hgrn_call(z, lb, gain, *, bsz, seq, heads, q_col, f_col, i_col, og_col, chunk, sub, unroll):
    tt = _tile(seq, 256)
    nt = seq // tt
    kern = functools.partial(_hgrn_kernel, tt=tt, chunk=chunk, sub=sub, unroll=unroll)

    def col_spec(col):
        return pl.BlockSpec((tt, HEAD_DIM), lambda b, h, t: (b * nt + t, col + h))

    head_spec = pl.BlockSpec((None, 1, HEAD_DIM), lambda b, h, t: (h, 0, 0))
    return pl.pallas_call(
        kern,
        grid=(bsz, heads, nt),
        in_specs=[col_spec(q_col), col_spec(f_col), col_spec(i_col), col_spec(og_col), head_spec, head_spec],
        out_specs=pl.BlockSpec((tt, HEAD_DIM), lambda b, h, t: (b * nt + t, h)),
        out_shape=jax.ShapeDtypeStruct((bsz * seq, heads * HEAD_DIM), BF16),
        scratch_shapes=[pltpu.VMEM((HEAD_DIM, HEAD_DIM), F32)],
        compiler_params=_params("parallel", "parallel", "arbitrary"),
        name=f"hgrn_c{chunk}",
    )(z, z, z, z, lb, gain)


def _hgrn(z, lb, gain, **kw):
    fast = functools.partial(_hgrn_call, chunk=HGRN_CHUNK, sub=HGRN_SUB, unroll=True, **kw)
    safe = functools.partial(_hgrn_call, chunk=HGRN_SAFE_CHUNK, sub=1, unroll=False, **kw)
    fast_ok = -(HGRN_SUB - 1) * jnp.log(jnp.min(lb)) <= HGRN_MAX_EXPONENT
    return lax.cond(fast_ok, fast, safe, z, lb, gain)


def _mixout_kernel(*refs, alpha, n_gate, gw):
    oa_ref, ob_ref = refs[0], refs[1]
    ga_refs = refs[2:2 + n_gate]
    gb_refs = refs[2 + n_gate:2 + 2 * n_gate]
    x_ref, wpa_ref, wpb_ref, wo_ref, g_ref, b_ref, o_ref = refs[2 + 2 * n_gate:]
    oa = oa_ref[...]
    ob = ob_ref[...]
    mix = None
    for c in range(n_gate):
        ya = _dot(oa, wpa_ref[:, c * gw:(c + 1) * gw])
        yb = _dot(ob, wpb_ref[:, c * gw:(c + 1) * gw])
        merged = (jax.nn.sigmoid(ga_refs[c][...].astype(F32)) * ya
                  + jax.nn.sigmoid(gb_refs[c][...].astype(F32)) * yb)
        part = _dot(merged.astype(BF16), wo_ref[c * gw:(c + 1) * gw, :])
        mix = part if mix is None else mix + part
    o_ref[...] = _layer_norm(alpha * x_ref[...] + mix, g_ref[...], b_ref[...])


def _mixout(o_a, o_b, z, x, wpa, wpb, wo, ln_g, ln_b, *, alpha, ga_off, gb_off):
    m, d = x.shape
    gw = math.gcd(math.gcd(ga_off, gb_off), d)
    n_gate = d // gw
    tm = _tile(m, 256)
    kern = functools.partial(_mixout_kernel, alpha=alpha, n_gate=n_gate, gw=gw)

    def gate_spec(off, c):
        return pl.BlockSpec((tm, gw), lambda i: (i, off // gw + c))

    def whole(a):
        return pl.BlockSpec(a.shape, lambda i: (0,) * a.ndim, pipeline_mode=pl.Buffered(1))

    def rows(a):
        return pl.BlockSpec((tm, a.shape[1]), lambda i: (i, 0))

    return pl.pallas_call(
        kern,
        grid=(m // tm,),
        in_specs=([rows(o_a), rows(o_b)]
                  + [gate_spec(ga_off, c) for c in range(n_gate)]
                  + [gate_spec(gb_off, c) for c in range(n_gate)]
                  + [rows(x), whole(wpa), whole(wpb), whole(wo), whole(ln_g), whole(ln_b)]),
        out_specs=pl.BlockSpec((tm, d), lambda i: (i, 0)),
        out_shape=jax.ShapeDtypeStruct((m, d), F32),
        compiler_params=_params("parallel"),
        name="mixout",
    )(o_a, o_b, *([z] * (2 * n_gate)), x, wpa, wpb, wo, ln_g, ln_b)


def _ffn_kernel(h_ref, wg_ref, wu_ref, wd_ref, g_ref, b_ref, o_ref, hb_ref, *, alpha, nf):
    f = pl.program_id(1)

    @pl.when(f == 0)
    def _():
        hb_ref[...] = h_ref[...].astype(BF16)
        o_ref[...] = jnp.zeros_like(o_ref)

    hb = hb_ref[...]
    gate = _dot(hb, wg_ref[...])
    up = _dot(hb, wu_ref[...])
    act = (gate * jax.nn.sigmoid(gate) * up).astype(BF16)
    o_ref[...] += _dot(act, wd_ref[...])

    @pl.when(f == nf - 1)
    def _():
        o_ref[...] = _layer_norm(alpha * h_ref[...] + o_ref[...], g_ref[...], b_ref[...])


def _ffn(h, wg, wu, wd, ln_g, ln_b, *, alpha):
    m, d = h.shape
    hidden = wg.shape[1]
    tm, tf = _tile(m, 512), _tile(hidden, 512)
    nf = hidden // tf
    kern = functools.partial(_ffn_kernel, alpha=alpha, nf=nf)
    vec = pl.BlockSpec((1, d), lambda i, f: (0, 0))
    return pl.pallas_call(
        kern,
        grid=(m // tm, nf),
        in_specs=[pl.BlockSpec((tm, d), lambda i, f: (i, 0)),
                  pl.BlockSpec((d, tf), lambda i, f: (0, f)),
                  pl.BlockSpec((d, tf), lambda i, f: (0, f)),
                  pl.BlockSpec((tf, d), lambda i, f: (f, 0)),
                  vec, vec],
        out_specs=pl.BlockSpec((tm, d), lambda i, f: (i, 0)),
        out_shape=jax.ShapeDtypeStruct((m, d), F32),
        scratch_shapes=[pltpu.VMEM((tm, d), BF16)],
        compiler_params=_params("parallel", "arbitrary"),
        name="ffn",
    )(h, wg, wu, wd, ln_g, ln_b)


def kernel(x, w_in, w_proj_a, w_proj_b, w_out, hgrn_norm_g, hgrn_lb_logits, ln1_g, ln1_b,
           w_gate_ffn, w_up_ffn, w_down_ffn, ln2_g, ln2_b):
    bsz, seq, d = x.shape
    depth = w_in.shape[0]
    a_width = w_proj_a.shape[1]
    b_vwidth = w_proj_b.shape[1]
    b_fwidth = hgrn_lb_logits.shape[1]
    assert a_width % HEAD_DIM == 0 and b_vwidth == b_fwidth and b_fwidth % HEAD_DIM == 0
    assert w_in.shape[2] == 3 * a_width + 2 * b_fwidth + 2 * b_vwidth + 2 * d
    a_heads = a_width // HEAD_DIM
    b_heads = b_fwidth // HEAD_DIM
    qa_col, ka_col, va_col = 0, a_heads, 2 * a_heads
    qb_col = 3 * a_heads
    fb_col, ib_col, ogb_col = qb_col + b_heads, qb_col + 2 * b_heads, qb_col + 3 * b_heads
    ga_off = 3 * a_width + 2 * b_fwidth + 2 * b_vwidth
    gb_off = ga_off + d
    alpha = (2.0 * depth) ** 0.25

    lb_all = jnp.cumsum(jax.nn.softmax(hgrn_lb_logits.astype(F32), axis=0), axis=0)

    h = x.reshape(bsz * seq, d)
    for l in range(depth):
        z = _inproj(h, w_in[l].astype(BF16))
        o_a = _moba(z, bsz, seq, a_heads, qa_col, ka_col, va_col)
        o_b = _hgrn(z, lb_all[l].reshape(b_heads, 1, HEAD_DIM),
                    hgrn_norm_g[l].astype(F32).reshape(b_heads, 1, HEAD_DIM),
                    bsz=bsz, seq=seq, heads=b_heads,
                    q_col=qb_col, f_col=fb_col, i_col=ib_col, og_col=ogb_col)
        h = _mixout(o_a, o_b, z, h, w_proj_a[l].astype(BF16), w_proj_b[l].astype(BF16),
                    w_out[l].astype(BF16), ln1_g[l].reshape(1, d), ln1_b[l].reshape(1, d),
                    alpha=alpha, ga_off=ga_off, gb_off=gb_off)
        h = _ffn(h, w_gate_ffn[l].astype(BF16), w_up_ffn[l].astype(BF16), w_down_ffn[l].astype(BF16),
                 ln2_g[l].reshape(1, d), ln2_b[l].reshape(1, d), alpha=alpha)
    return h.reshape(bsz, seq, d)
```

```python
import functools
import math

import jax
import jax.numpy as jnp
from jax import lax
from jax.experimental import pallas as pl
from jax.experimental.pallas import tpu as pltpu

F32 = jnp.float32
BF16 = jnp.bfloat16

HEAD_DIM = 128
MOBA_BLOCK = 256
MOBA_TOPK = 3
MOBA_GROUP = 2
MOBA_ONES_ROWS = 16
LOG2_E = 1.4426950408889634
LN_EPS = 1e-5
RMS_EPS = 1e-6
NEG_INF = float("-inf")

HGRN_CHUNK = 64
HGRN_SUB = 16
HGRN_SAFE_CHUNK = 8
HGRN_MAX_EXPONENT = 60.0

VMEM_LIMIT_BYTES = 56 * 1024 * 1024

NT_DIMS = (((1,), (1,)), ((), ()))
TN_DIMS = (((0,), (0,)), ((), ()))


def _tile(n, pref):
    if n <= pref:
        return n
    t = pref - pref % 128
    while t >= 128:
        if n % t == 0:
            return t
        t -= 128
    raise ValueError(f"no 128-aligned tile for {n}")


def _params(*sem):
    return pltpu.CompilerParams(dimension_semantics=sem, vmem_limit_bytes=VMEM_LIMIT_BYTES)


def _dot(a, b):
    return jnp.dot(a, b, preferred_element_type=F32)


def _layer_norm(pre, g, b):
    mu = jnp.mean(pre, axis=-1, keepdims=True)
    d = pre - mu
    var = jnp.mean(d * d, axis=-1, keepdims=True)
    return d * lax.rsqrt(var + LN_EPS) * g + b


def _inproj_kernel(x_ref, w_ref, o_ref, xb_ref):
    @pl.when(pl.program_id(1) == 0)
    def _():
        xb_ref[...] = x_ref[...].astype(BF16)

    o_ref[...] = _dot(xb_ref[...], w_ref[...]).astype(o_ref.dtype)


def _inproj(x, w):
    m, k = x.shape
    n = w.shape[1]
    tm, tn = _tile(m, 1024), _tile(n, 1024)
    return pl.pallas_call(
        _inproj_kernel,
        grid=(m // tm, n // tn),
        in_specs=[pl.BlockSpec((tm, k), lambda i, j: (i, 0)),
                  pl.BlockSpec((k, tn), lambda i, j: (0, j))],
        out_specs=pl.BlockSpec((tm, tn), lambda i, j: (i, j)),
        out_shape=jax.ShapeDtypeStruct((m, n), BF16),
        scratch_shapes=[pltpu.VMEM((tm, k), BF16)],
        compiler_params=_params("parallel", "arbitrary"),
        name="inproj",
    )(x, w)


def _moba_kernel(q_ref, k_ref, v_ref, o_ref, kmean_ref, vt_ref, sel_ref, sa_ref, sb_ref, pa_ref, pb_ref,
                 acc_ref, *, nb, nbp, log2_scale):
    i = pl.program_id(2)
    blk = MOBA_BLOCK

    @pl.when(i == 0)
    def _():
        kmean_ref[...] = jnp.zeros_like(kmean_ref)

        def prep(n, carry):
            off = pl.multiple_of(n * blk, blk)
            kf = k_ref[pl.ds(off, blk), :].astype(F32)
            kmean_ref[pl.ds(n, 1), :] = jnp.sum(kf, axis=0, keepdims=True) * (1.0 / blk)
            vt_ref[n, :HEAD_DIM, :] = v_ref[pl.ds(off, blk), :].astype(F32).T.astype(BF16)
            vt_ref[n, HEAD_DIM:, :] = jnp.ones((MOBA_ONES_ROWS, blk), BF16)
            return carry

        lax.fori_loop(0, nb, prep, 0)

    qt32 = q_ref[...].astype(F32).T
    qt = qt32.astype(BF16)
    qts = (qt32 * log2_scale).astype(BF16)

    km = kmean_ref[...]
    km_hi = km.astype(BF16)
    km_lo = (km - km_hi.astype(F32)).astype(BF16)
    gate = _dot(km_hi, qt) + _dot(km_lo, qt)
    row = lax.broadcasted_iota(jnp.int32, (nbp, blk), 0)
    g = jnp.where(row < i, gate, NEG_INF)
    sel = jnp.zeros((nbp, blk), dtype=jnp.bool_)
    for _ in range(MOBA_TOPK):
        mx = jnp.max(g, axis=0, keepdims=True)
        idx = jnp.min(jnp.where(g == mx, row, nbp), axis=0, keepdims=True)
        pick = (row == idx) & (mx > NEG_INF)
        sel = sel | pick
        g = jnp.where(pick, NEG_INF, g)
    sel_ref[...] = sel.astype(F32)

    kr = lax.broadcasted_iota(jnp.int32, (blk, blk), 0)
    qc = lax.broadcasted_iota(jnp.int32, (blk, blk), 1)
    grp = MOBA_GROUP

    def blocks(n):
        ids = [i - n * grp - c for c in range(grp)]
        return [jnp.maximum(j, 0) for j in ids], [(j >= 0).astype(F32) for j in ids]

    def scores_into(buf, n):
        for c, j in enumerate(blocks(n)[0]):
            buf[c] = _dot(k_ref[pl.ds(pl.multiple_of(j * blk, blk), blk), :], qts)

    def softmax_into(pbuf, sbuf, n, m_old):
        first = m_old is None
        js, oks = blocks(n)
        chosen = [None if (first and c == 0) else sel_ref[pl.ds(js[c], 1), :] * oks[c] > 0.0
                  for c in range(grp)]

        def tile(c):
            return jnp.where(kr <= qc, sbuf[c], NEG_INF) if chosen[c] is None else sbuf[c]

        m_new = m_old
        for c in range(grp):
            cm = jnp.max(tile(c), axis=0, keepdims=True)
            if chosen[c] is not None:
                cm = jnp.where(chosen[c], cm, NEG_INF)
            m_new = cm if m_new is None else jnp.maximum(m_new, cm)
        for c in range(grp):
            mb = m_new if chosen[c] is None else jnp.where(chosen[c], m_new, jnp.inf)
            pbuf[c] = jnp.exp2(tile(c) - mb).astype(BF16)
        alpha = jnp.ones((1, blk), F32) if first else jnp.exp2(m_old - m_new)
        return m_new, alpha

    def accumulate(pbuf, n, alpha):
        pv = None
        for c, j in enumerate(blocks(n)[0]):
            d = _dot(vt_ref[j], pbuf[c])
            pv = d if pv is None else pv + d
        acc_ref[...] = alpha * acc_ref[...] + pv

    acc_ref[...] = jnp.zeros_like(acc_ref)
    scores_into(sa_ref, 0)
    scores_into(sb_ref, 1)
    m, alpha = softmax_into(pa_ref, sa_ref, 0, None)

    def pair(t, carry):
        m, alpha = carry
        n = 1 + 2 * t
        scores_into(sa_ref, n + 1)
        accumulate(pa_ref, n - 1, alpha)
        m, alpha = softmax_into(pb_ref, sb_ref, n, m)
        accumulate(pb_ref, n, alpha)
        scores_into(sb_ref, n + 2)
        return softmax_into(pa_ref, sa_ref, n + 1, m)

    n_groups = (i + grp) // grp
    n_pairs = n_groups // 2
    m, alpha = lax.fori_loop(0, n_pairs, pair, (m, alpha))
    accumulate(pa_ref, 2 * n_pairs, alpha)
    num = acc_ref[:HEAD_DIM, :]
    den = acc_ref[HEAD_DIM:HEAD_DIM + 1, :]
    o_ref[...] = (num / den).T.astype(o_ref.dtype)


def _moba(z, bsz, seq, heads, q_col, k_col, v_col):
    blk = MOBA_BLOCK
    assert seq % blk == 0
    nb = seq // blk
    nbp = -(-nb // 8) * 8
    acc_rows = HEAD_DIM + MOBA_ONES_ROWS
    z3 =z.reshape(bsz, seq, z.shape[1])
    kern = functools.partial(_moba_kernel, nb=nb, nbp=nbp, log2_scale=HEAD_DIM ** -0.5 * LOG2_E)
    return pl.pallas_call(
        kern,
        grid=(bsz, heads, nb),
        in_specs=[pl.BlockSpec((blk, HEAD_DIM), lambda b, h, i: (b * nb + i, q_col + h)),
                  pl.BlockSpec((None, seq, HEAD_DIM), lambda b, h, i: (b, 0, k_col + h)),
                  pl.BlockSpec((None, seq, HEAD_DIM), lambda b, h, i: (b, 0, v_col + h))],
        out_specs=pl.BlockSpec((blk, HEAD_DIM), lambda b, h, i: (b * nb + i, h)),
        out_shape=jax.ShapeDtypeStruct((bsz * seq, heads * HEAD_DIM), BF16),
        scratch_shapes=[pltpu.VMEM((nbp, HEAD_DIM), F32),
                        pltpu.VMEM((nb, acc_rows, blk), BF16),
                        pltpu.VMEM((nbp, blk), F32),
                        pltpu.VMEM((MOBA_GROUP, blk, blk), F32),
                        pltpu.VMEM((MOBA_GROUP, blk, blk), F32),
                        pltpu.VMEM((MOBA_GROUP, blk, blk), BF16),
                        pltpu.VMEM((MOBA_GROUP, blk, blk), BF16),
                        pltpu.VMEM((acc_rows, blk), F32)],
        compiler_params=_params("parallel", "parallel", "arbitrary"),
        name="moba",
    )(z, z3, z3)


def _hgrn_kernel(q_ref, f_ref, i_ref, og_ref, lb_ref, gain_ref, o_ref, st_ref, *, tt, chunk, sub, unroll):
    @pl.when(pl.program_id(2) == 0)
    def _():
        st_ref[...] = jnp.zeros_like(st_ref)

    n_blocks = chunk // sub
    stack = n_blocks * chunk
    lb = lb_ref[...]
    gain = gain_ref[...]
    tri = (lax.broadcasted_iota(jnp.int32, (chunk, chunk), 0)
           >= lax.broadcasted_iota(jnp.int32, (chunk, chunk), 1)).astype(BF16)
    r_i = lax.broadcasted_iota(jnp.int32, (chunk, stack), 0)
    c_i = lax.broadcasted_iota(jnp.int32, (chunk, stack), 1)
    keep = ((c_i // chunk) == (r_i // sub)) & ((c_i % chunk) <= r_i)
    row_c = lax.broadcasted_iota(jnp.int32, (chunk, HEAD_DIM), 0)

    def one_chunk(n, carry):
        off = pl.multiple_of(n * chunk, chunk)
        xq = q_ref[pl.ds(off, chunk), :].astype(F32)
        qs = xq * jax.nn.sigmoid(xq)
        f = lb + (1.0 - lb) * jax.nn.sigmoid(f_ref[pl.ds(off, chunk), :].astype(F32))
        logf = jnp.log(f)
        kk = 1.0 - f
        vb = i_ref[pl.ds(off, chunk), :]

        hi = logf.astype(BF16)
        r1 = logf - hi.astype(F32)
        mid = r1.astype(BF16)
        lo = (r1 - mid.astype(F32)).astype(BF16)
        b = _dot(tri, hi) + _dot(tri, mid) + _dot(tri, lo)
        b_last = b[chunk - 1:chunk, :]

        st = st_ref[...]
        qe = (qs * jnp.exp(b)).astype(BF16)
        o_inter = lax.dot_general(qe, st.astype(BF16), NT_DIMS, preferred_element_type=F32)

        ref_rows = jnp.concatenate(
            [jnp.broadcast_to(b[blk * sub:blk * sub + 1, :], (sub, HEAD_DIM)) for blk in range(n_blocks)],
            axis=0)
        qt = (qs * jnp.exp(b - ref_rows)).astype(BF16)
        kh = jnp.concatenate(
            [(kk * jnp.exp(jnp.where(row_c < (blk + 1) * sub, b[blk * sub:blk * sub + 1, :] - b, NEG_INF))
              ).astype(BF16) for blk in range(n_blocks)], axis=0)
        res = lax.dot_general(qt, kh, NT_DIMS, preferred_element_type=F32)
        a = jnp.where(keep, res, 0.0).astype(BF16)
        o = o_inter + _dot(a, jnp.concatenate([vb] * n_blocks, axis=0))

        kd = (kk * jnp.exp(b_last - b)).astype(BF16)
        st_ref[...] = st * jnp.exp(b_last) + lax.dot_general(vb, kd, TN_DIMS, preferred_element_type=F32)

        ms = jnp.mean(o * o, axis=-1, keepdims=True)
        on = o * lax.rsqrt(ms + RMS_EPS) * gain
        og = og_ref[pl.ds(off, chunk), :].astype(F32)
        o_ref[pl.ds(off, chunk), :] = (on * jax.nn.sigmoid(og)).astype(o_ref.dtype)
        return carry

    lax.fori_loop(0, tt // chunk, one_chunk, 0, unroll=unroll)


def _hgrn_call(z, lb, gain, *, bsz, seq, heads, q_col, f_col, i_col, og_col, chunk, sub, unroll):
    tt = _tile(seq, 256)
    nt = seq // tt
    kern = functools.partial(_hgrn_kernel, tt=tt, chunk=chunk, sub=sub, unroll=unroll)

    def col_spec(col):
        return pl.BlockSpec((tt, HEAD_DIM), lambda b, h, t: (b * nt + t, col + h))

    head_spec = pl.BlockSpec((None, 1, HEAD_DIM), lambda b, h, t: (h, 0, 0))
    return pl.pallas_call(
        kern,
        grid=(bsz, heads, nt),
        in_specs=[col_spec(q_col), col_spec(f_col), col_spec(i_col), col_spec(og_col), head_spec, head_spec],
        out_specs=pl.BlockSpec((tt, HEAD_DIM), lambda b, h, t: (b * nt + t, h)),
        out_shape=jax.ShapeDtypeStruct((bsz * seq, heads * HEAD_DIM), BF16),
        scratch_shapes=[pltpu.VMEM((HEAD_DIM, HEAD_DIM), F32)],
        compiler_params=_params("parallel", "parallel", "arbitrary"),
        name=f"hgrn_c{chunk}",
    )(z, z, z, z, lb, gain)


def _hgrn(z, lb, gain, **kw):
    fast = functools.partial(_hgrn_call, chunk=HGRN_CHUNK, sub=HGRN_SUB, unroll=True, **kw)
    safe = functools.partial(_hgrn_call, chunk=HGRN_SAFE_CHUNK, sub=1, unroll=False, **kw)
    fast_ok = -(HGRN_SUB - 1) * jnp.log(jnp.min(lb)) <= HGRN_MAX_EXPONENT
    return lax.cond(fast_ok, fast, safe, z, lb, gain)


def _mixout_kernel(*refs, alpha, n_gate, gw):
    oa_ref, ob_ref = refs[0], refs[1]
    ga_refs = refs[2:2 + n_gate]
    gb_refs = refs[2 + n_gate:2 + 2 * n_gate]
    x_ref, wpa_ref, wpb_ref, wo_ref, g_ref, b_ref, o_ref = refs[2 + 2 * n_gate:]
    oa = oa_ref[...]
    ob = ob_ref[...]
    mix = None
    for c in range(n_gate):
        ya = _dot(oa, wpa_ref[:, c * gw:(c + 1) * gw])
        yb = _dot(ob, wpb_ref[:, c * gw:(c + 1) * gw])
        merged = (jax.nn.sigmoid(ga_refs[c][...].astype(F32)) * ya
                  + jax.nn.sigmoid(gb_refs[c][...].astype(F32)) * yb)
        part = _dot(merged.astype(BF16), wo_ref[c * gw:(c + 1) * gw, :])
        mix = part if mix is None else mix + part
    o_ref[...] = _layer_norm(alpha * x_ref[...] + mix, g_ref[...], b_ref[...])


def _mixout(o_a, o_b, z, x, wpa, wpb, wo, ln_g, ln_b, *, alpha, ga_off, gb_off):
    m, d = x.shape
    gw = math.gcd(math.gcd(ga_off, gb_off), d)
    n_gate = d // gw
    tm = _tile(m, 256)
    kern = functools.partial(_mixout_kernel, alpha=alpha, n_gate=n_gate, gw=gw)

    def gate_spec(off, c):
        return pl.BlockSpec((tm, gw), lambda i: (i, off // gw + c))

    def whole(a):
        return pl.BlockSpec(a.shape, lambda i: (0,) * a.ndim, pipeline_mode=pl.Buffered(1))

    def rows(a):
        return pl.BlockSpec((tm, a.shape[1]), lambda i: (i, 0))

    return pl.pallas_call(
        kern,
        grid=(m // tm,),
        in_specs=([rows(o_a), rows(o_b)]
                  + [gate_spec(ga_off, c) for c in range(n_gate)]
                  + [gate_spec(gb_off, c) for c in range(n_gate)]
                  + [rows(x), whole(wpa), whole(wpb), whole(wo), whole(ln_g), whole(ln_b)]),
        out_specs=pl.BlockSpec((tm, d), lambda i: (i, 0)),
        out_shape=jax.ShapeDtypeStruct((m, d), F32),
        compiler_params=_params("parallel"),
        name="mixout",
    )(o_a, o_b, *([z] * (2 * n_gate)), x, wpa, wpb, wo, ln_g, ln_b)


def _ffn_kernel(h_ref, wg_ref, wu_ref, wd_ref, g_ref, b_ref, o_ref, hb_ref, *, alpha, nf):
    f = pl.program_id(1)

    @pl.when(f == 0)
    def _():
        hb_ref[...] = h_ref[...].astype(BF16)
        o_ref[...] = jnp.zeros_like(o_ref)

    hb = hb_ref[...]
    gate = _dot(hb, wg_ref[...])
    up = _dot(hb, wu_ref[...])
    act = (gate * jax.nn.sigmoid(gate) * up).astype(BF16)
    o_ref[...] += _dot(act, wd_ref[...])

    @pl.when(f == nf - 1)
    def _():
        o_ref[...] = _layer_norm(alpha * h_ref[...] + o_ref[...], g_ref[...], b_ref[...])


def _ffn(h, wg, wu, wd, ln_g, ln_b, *, alpha):
    m, d = h.shape
    hidden = wg.shape[1]
    tm, tf = _tile(m, 512), _tile(hidden, 512)
    nf = hidden // tf
    kern = functools.partial(_ffn_kernel, alpha=alpha, nf=nf)
    vec = pl.BlockSpec((1, d), lambda i, f: (0, 0))
    return pl.pallas_call(
        kern,
        grid=(m // tm, nf),
        in_specs=[pl.BlockSpec((tm, d), lambda i, f: (i, 0)),
                  pl.BlockSpec((d, tf), lambda i, f: (0, f)),
                  pl.BlockSpec((d, tf), lambda i, f: (0, f)),
                  pl.BlockSpec((tf, d), lambda i, f: (f, 0)),
                  vec, vec],
        out_specs=pl.BlockSpec((tm, d), lambda i, f: (i, 0)),
        out_shape=jax.ShapeDtypeStruct((m, d), F32),
        scratch_shapes=[pltpu.VMEM((tm, d), BF16)],
        compiler_params=_params("parallel", "arbitrary"),
        name="ffn",
    )(h, wg, wu, wd, ln_g, ln_b)


def kernel(x, w_in, w_proj_a, w_proj_b, w_out, hgrn_norm_g, hgrn_lb_logits, ln1_g, ln1_b,
           w_gate_ffn, w_up_ffn, w_down_ffn, ln2_g, ln2_b):
    bsz, seq, d = x.shape
    depth = w_in.shape[0]
    a_width = w_proj_a.shape[1]
    b_vwidth = w_proj_b.shape[1]
    b_fwidth = hgrn_lb_logits.shape[1]
    assert a_width % HEAD_DIM == 0 and b_vwidth == b_fwidth and b_fwidth % HEAD_DIM == 0
    assert w_in.shape[2] == 3 * a_width + 2 * b_fwidth + 2 * b_vwidth + 2 * d
    a_heads = a_width // HEAD_DIM
    b_heads = b_fwidth // HEAD_DIM
    qa_col, ka_col, va_col = 0, a_heads, 2 * a_heads
    qb_col = 3 * a_heads
    fb_col, ib_col, ogb_col = qb_col + b_heads, qb_col + 2 * b_heads, qb_col + 3 * b_heads
    ga_off = 3 * a_width + 2 * b_fwidth + 2 * b_vwidth
    gb_off = ga_off + d
    alpha = (2.0 * depth) ** 0.25

    lb_all = jnp.cumsum(jax.nn.softmax(hgrn_lb_logits.astype(F32), axis=0), axis=0)

    h = x.reshape(bsz * seq, d)
    for l in range(depth):
        z = _inproj(h, w_in[l].astype(BF16))
        o_a = _moba(z, bsz, seq, a_heads, qa_col, ka_col, va_col)
        o_b = _hgrn(z, lb_all[l].reshape(b_heads, 1, HEAD_DIM),
                    hgrn_norm_g[l].astype(F32).reshape(b_heads, 1, HEAD_DIM),
                    bsz=bsz, seq=seq, heads=b_heads,
                    q_col=qb_col, f_col=fb_col, i_col=ib_col, og_col=ogb_col)
        h = _mixout(o_a, o_b, z, h, w_proj_a[l].astype(BF16), w_proj_b[l].astype(BF16),
                    w_out[l].astype(BF16), ln1_g[l].reshape(1, d), ln1_b[l].reshape(1, d),
                    alpha=alpha, ga_off=ga_off, gb_off=gb_off)
        h = _ffn(h, w_gate_ffn[l].astype(BF16), w_up_ffn[l].astype(BF16), w_down_ffn[l].astype(BF16),
                 ln2_g[l].reshape(1, d), ln2_b[l].reshape(1, d), alpha=alpha)
    return h.reshape(bsz, seq, d)
```

```python
import functools
import math

import jax
import jax.numpy as jnp
from jax import lax
from jax.experimental import pallas as pl
from jax.experimental.pallas import tpu as pltpu

F32 = jnp.float32
BF16 = jnp.bfloat16

HEAD_DIM = 128
MOBA_BLOCK = 256
MOBA_TOPK = 3
MOBA_GROUP = 2
MOBA_HEADS_PER_STEP = 2
MOBA_ONES_ROWS = 16
LOG2_E = 1.4426950408889634
LN_EPS = 1e-5
RMS_EPS = 1e-6
NEG_INF = float("-inf")

HGRN_CHUNK = 64
HGRN_SUB = 16
HGRN_SPAN = 256
HGRN_HEADS_PER_STEP = 4
HGRN_SAFE_CHUNK = 8
HGRN_MAX_EXPONENT = 60.0

VMEM_LIMIT_BYTES = 56 * 1024 * 1024

NT_DIMS = (((1,), (1,)), ((), ()))
TN_DIMS = (((0,), (0,)), ((), ()))


def _tile(n, pref):
    if n <= pref:
        return n
    t = pref - pref % 128
    while t >= 128:
        if n % t == 0:
            return t
        t -= 128
    raise ValueError(f"no 128-aligned tile for {n}")


def _params(*sem):
    return pltpu.CompilerParams(dimension_semantics=sem, vmem_limit_bytes=VMEM_LIMIT_BYTES)


def _dot(a, b):
    return jnp.dot(a, b, preferred_element_type=F32)


def _layer_norm(pre, g, b):
    mu = jnp.mean(pre, axis=-1, keepdims=True)
    d = pre - mu
    var = jnp.mean(d * d, axis=-1, keepdims=True)
    return d * lax.rsqrt(var + LN_EPS) * g + b


def _inproj_kernel(x_ref, w_ref, o_ref, xb_ref):
    @pl.when(pl.program_id(1) == 0)
    def _():
        xb_ref[...] = x_ref[...].astype(BF16)

    o_ref[...] = _dot(xb_ref[...], w_ref[...]).astype(o_ref.dtype)


def _inproj(x, w):
    m, k = x.shape
    n = w.shape[1]
    tm, tn = _tile(m, 1024), _tile(n, 1024)
    return pl.pallas_call(
        _inproj_kernel,
        grid=(m // tm, n // tn),
        in_specs=[pl.BlockSpec((tm, k), lambda i, j: (i, 0)),
                  pl.BlockSpec((k, tn), lambda i, j: (0, j))],
        out_specs=pl.BlockSpec((tm, tn), lambda i, j: (i, j)),
        out_shape=jax.ShapeDtypeStruct((m, n), BF16),
        scratch_shapes=[pltpu.VMEM((tm, k), BF16)],
        compiler_params=_params("parallel", "arbitrary"),
        name="inproj",
    )(x, w)


def _moba_kernel(q_ref, k_ref, v_ref, o_ref, kmean_ref, vt_ref, sel_ref, sa_ref, sb_ref, pa_ref, pb_ref,
                 acc_ref, *, nb, nbp, log2_scale, hp):
    i = pl.program_id(2)
    blk = MOBA_BLOCK
    grp = MOBA_GROUP
    heads = range(hp)
    cols = [slice(h * HEAD_DIM, (h + 1) * HEAD_DIM) for h in heads]

    @pl.when(i == 0)
    def _():
        kmean_ref[...] = jnp.zeros_like(kmean_ref)

        def prep(n, carry):
            rows = pl.ds(pl.multiple_of(n * blk, blk), blk)
            for h in heads:
                kf = k_ref[rows, cols[h]].astype(F32)
                kmean_ref[h, pl.ds(n, 1), :] = jnp.sum(kf, axis=0, keepdims=True) * (1.0 / blk)
                vt_ref[h, n, :HEAD_DIM, :] = v_ref[rows, cols[h]].astype(F32).T.astype(BF16)
                vt_ref[h, n, HEAD_DIM:, :] = jnp.ones((MOBA_ONES_ROWS, blk), BF16)
            return carry

        lax.fori_loop(0, nb, prep, 0)

    row = lax.broadcasted_iota(jnp.int32, (nbp, blk), 0)
    qts = []
    for h in heads:
        qt32 = q_ref[:, cols[h]].astype(F32).T
        qt = qt32.astype(BF16)
        qts.append((qt32 * log2_scale).astype(BF16))
        km = kmean_ref[h]
        km_hi = km.astype(BF16)
        km_lo = (km - km_hi.astype(F32)).astype(BF16)
        gate = _dot(km_hi, qt) + _dot(km_lo, qt)
        g = jnp.where(row < i, gate, NEG_INF)
        sel = jnp.zeros((nbp, blk), dtype=jnp.bool_)
        for _ in range(MOBA_TOPK):
            mx = jnp.max(g, axis=0, keepdims=True)
            idx = jnp.min(jnp.where(g == mx, row, nbp), axis=0, keepdims=True)
            pick = (row == idx) & (mx > NEG_INF)
            sel = sel | pick
            g = jnp.where(pick, NEG_INF, g)
        sel_ref[h] = sel.astype(F32)

    kr = lax.broadcasted_iota(jnp.int32, (blk, blk), 0)
    qc = lax.broadcasted_iota(jnp.int32, (blk, blk), 1)

    def blocks(n):
        ids = [i - n * grp - c for c in range(grp)]
        return [jnp.maximum(j, 0) for j in ids], [(j >= 0).astype(F32) for j in ids]

    def scores_into(buf, n):
        for h in heads:
            for c, j in enumerate(blocks(n)[0]):
                buf[h, c] = _dot(k_ref[pl.ds(pl.multiple_of(j * blk, blk), blk), cols[h]], qts[h])

    def softmax_into(pbuf, sbuf, n, m_old):
        first = m_old is None
        js, oks = blocks(n)
        m_out, alpha_out = [], []
        for h in heads:
            chosen = [None if (first and c == 0) else sel_ref[h, pl.ds(js[c], 1), :] * oks[c] > 0.0
                      for c in range(grp)]

            def tile(c):
                return jnp.where(kr <= qc, sbuf[h, c], NEG_INF) if chosen[c] is None else sbuf[h, c]

            m_new = None if first else m_old[h]
            for c in range(grp):
                cm = jnp.max(tile(c), axis=0, keepdims=True)
                if chosen[c] is not None:
                    cm = jnp.where(chosen[c], cm, NEG_INF)
                m_new = cm if m_new is None else jnp.maximum(m_new, cm)
            for c in range(grp):
                mb = m_new if chosen[c] is None else jnp.where(chosen[c], m_new, jnp.inf)
                pbuf[h, c] = jnp.exp2(tile(c) - mb).astype(BF16)
            m_out.append(m_new)
            alpha_out.append(jnp.ones((1, blk), F32) if first else jnp.exp2(m_old[h] - m_new))
        return m_out, alpha_out

    def accumulate(pbuf, n, alpha):
        for h in heads:
            pv = None
            for c, j in enumerate(blocks(n)[0]):
                d = _dot(vt_ref[h, j], pbuf[h, c])
                pv = d if pv is None else pv + d
            acc_ref[h] = alpha[h] * acc_ref[h] + pv

    acc_ref[...] = jnp.zeros_like(acc_ref)
    scores_into(sa_ref, 0)
    scores_into(sb_ref, 1)
    m, alpha = softmax_into(pa_ref, sa_ref, 0, None)

    def pair(t, carry):
        m, alpha = list(carry[:hp]), list(carry[hp:])
        n = 1 + 2 * t
        scores_into(sa_ref, n + 1)
        accumulate(pa_ref, n - 1, alpha)
        m, alpha = softmax_into(pb_ref, sb_ref, n, m)
        scores_into(sb_ref, n + 2)
        accumulate(pb_ref, n, alpha)
        m, alpha = softmax_into(pa_ref, sa_ref, n + 1, m)
        return tuple(m) + tuple(alpha)

    n_groups = (i + grp) // grp
    n_pairs = n_groups // 2
    carry = lax.fori_loop(0, n_pairs, pair, tuple(m) + tuple(alpha))
    accumulate(pa_ref, 2 * n_pairs, list(carry[hp:]))
    for h in heads:
        num = acc_ref[h, :HEAD_DIM, :]
        den = acc_ref[h, HEAD_DIM:HEAD_DIM + 1, :]
        o_ref[:, cols[h]] = (num / den).T.astype(o_ref.dtype)


def _moba(z, bsz, seq, heads, q_col, k_col, v_col):
    blk = MOBA_BLOCK
    assert seq % blk == 0
    nb = seq // blk
    nbp = -(-nb // 8) * 8
    acc_rows = HEAD_DIM + MOBA_ONES_ROWS
    hp = math.gcd(heads, MOBA_HEADS_PER_STEP)
    assert all(col % hp == 0 for col in (q_col, k_col, v_col))
    width = hp * HEAD_DIM
    z3 = z.reshape(bsz, seq, z.shape[1])
    kern = functools.partial(_moba_kernel, nb=nb, nbp=nbp, log2_scale=HEAD_DIM ** -0.5 * LOG2_E, hp=hp)
    return pl.pallas_call(
        kern,
        grid=(bsz, heads // hp, nb),
        in_specs=[pl.BlockSpec((blk, width), lambda b, h, i: (b * nb + i, q_col // hp + h)),
                  pl.BlockSpec((None, seq, width), lambda b, h, i: (b, 0, k_col // hp + h)),
                  pl.BlockSpec((None, seq, width), lambda b, h, i: (b, 0, v_col // hp + h))],
        out_specs=pl.BlockSpec((blk, width), lambda b, h, i: (b * nb + i, h)),
        out_shape=jax.ShapeDtypeStruct((bsz * seq, heads * HEAD_DIM), BF16),
        scratch_shapes=[pltpu.VMEM((hp, nbp, HEAD_DIM), F32),
                        pltpu.VMEM((hp, nb, acc_rows, blk), BF16),
                        pltpu.VMEM((hp, nbp, blk), F32),
                        pltpu.VMEM((hp, MOBA_GROUP, blk, blk), F32),
                        pltpu.VMEM((hp, MOBA_GROUP, blk, blk), F32),
                        pltpu.VMEM((hp, MOBA_GROUP, blk, blk), BF16),
                        pltpu.VMEM((hp, MOBA_GROUP, blk, blk), BF16),
                        pltpu.VMEM((hp, acc_rows, blk), F32)],
        compiler_params=_params("parallel", "parallel", "arbitrary"),
        name="moba",
    )(z, z3, z3)


def _hgrn_kernel(q_ref, f_ref, i_ref, og_ref, lb_ref, gain_ref, o_ref, st_ref, *, tt, chunk, sub, unroll):
    @pl.when(pl.program_id(2) == 0)
    def _():
        st_ref[...] = jnp.zeros_like(st_ref)

    n_blocks = chunk // sub
    stack = n_blocks * chunk
    lb = lb_ref[...]
    gain = gain_ref[...]
    tri = (lax.broadcasted_iota(jnp.int32, (chunk, chunk), 0)
           >= lax.broadcasted_iota(jnp.int32, (chunk, chunk), 1)).astype(BF16)
    r_i = lax.broadcasted_iota(jnp.int32, (chunk, stack), 0)
    c_i = lax.broadcasted_iota(jnp.int32, (chunk, stack), 1)
    keep = ((c_i // chunk) == (r_i // sub)) & ((c_i % chunk) <= r_i)
    row_c = lax.broadcasted_iota(jnp.int32, (chunk, HEAD_DIM), 0)

    def one_chunk(n, carry):
        off = pl.multiple_of(n * chunk, chunk)
        xq = q_ref[pl.ds(off, chunk), :].astype(F32)
        qs = xq * jax.nn.sigmoid(xq)
        f = lb + (1.0 - lb) * jax.nn.sigmoid(f_ref[pl.ds(off, chunk), :].astype(F32))
        logf = jnp.log(f)
        kk = 1.0 - f
        vb = i_ref[pl.ds(off, chunk), :]

        hi = logf.astype(BF16)
        r1 = logf - hi.astype(F32)
        mid = r1.astype(BF16)
        lo = (r1 - mid.astype(F32)).astype(BF16)
        b = _dot(tri, hi) + _dot(tri, mid) + _dot(tri, lo)
        b_last = b[chunk - 1:chunk, :]

        st = st_ref[...]
        qe = (qs * jnp.exp(b)).astype(BF16)
        o_inter = lax.dot_general(qe, st.astype(BF16), NT_DIMS, preferred_element_type=F32)

        ref_rows = jnp.concatenate(
            [jnp.broadcast_to(b[blk * sub:blk * sub + 1, :], (sub, HEAD_DIM)) for blk in range(n_blocks)],
            axis=0)
        qt = (qs * jnp.exp(b - ref_rows)).astype(BF16)
        kh = jnp.concatenate(
            [(kk * jnp.exp(jnp.where(row_c < (blk + 1) * sub, b[blk * sub:blk * sub + 1, :] - b, NEG_INF))
              ).astype(BF16) for blk in range(n_blocks)], axis=0)
        res = lax.dot_general(qt, kh, NT_DIMS, preferred_element_type=F32)
        a = jnp.where(keep, res, 0.0).astype(BF16)
        o = o_inter + _dot(a, jnp.concatenate([vb] * n_blocks, axis=0))

        kd = (kk * jnp.exp(b_last - b)).astype(BF16)
        st_ref[...] = st * jnp.exp(b_last) + lax.dot_general(vb, kd, TN_DIMS, preferred_element_type=F32)

        ms = jnp.mean(o * o, axis=-1, keepdims=True)
        on = o * lax.rsqrt(ms + RMS_EPS) * gain
        og = og_ref[pl.ds(off, chunk), :].astype(F32)
        o_ref[pl.ds(off, chunk), :] = (on * jax.nn.sigmoid(og)).astype(o_ref.dtype)
        return carry

    lax.fori_loop(0, tt // chunk, one_chunk, 0, unroll=unroll)


def _sigmoid(x):
    return 0.5 * jnp.tanh(0.5 * x) + 0.5


def _rows_to_blocks(rows, sub):
    return jnp.concatenate([jnp.broadcast_to(rows[k:k + 1, :], (sub, HEAD_DIM)) for k in range(rows.shape[0])],
                           axis=0)


def _hgrn_staged_kernel(q_ref, f_ref, i_ref, og_ref, lb_ref, gain_ref, o_ref, st_ref, *, tt, span, chunk, sub, hp):
    @pl.when(pl.program_id(2) == 0)
    def _():
        st_ref[...] = jnp.zeros_like(st_ref)

    n_chunks = span // chunk
    nblk = chunk // sub
    n_all = span // sub
    stack = nblk * chunk
    r_s = lax.broadcasted_iota(jnp.int32, (span, span), 0)
    c_s = lax.broadcasted_iota(jnp.int32, (span, span), 1)
    tri = ((r_s >= c_s) & (r_s // chunk == c_s // chunk)).astype(BF16)
    r_i = lax.broadcasted_iota(jnp.int32, (chunk, stack), 0)
    c_i = lax.broadcasted_iota(jnp.int32, (chunk, stack), 1)
    keep = ((c_i // chunk) == (r_i // sub)) & ((c_i % chunk) <= r_i)
    pair = lax.broadcasted_iota(jnp.int32, (nblk * nblk, HEAD_DIM), 0)
    pair_ok = (pair % nblk) <= (pair // nblk)
    heads = range(hp)
    cols = [slice(h * HEAD_DIM, (h + 1) * HEAD_DIM) for h in heads]
    f_mid = [0.5 + 0.5 * lb_ref[h] for h in heads]
    f_amp = [0.5 - 0.5 * lb_ref[h] for h in heads]

    def one_span(sidx, carry):
        rows = pl.ds(pl.multiple_of(sidx * span, span), span)

        qs, kk, vb, b = [], [], [], []
        for h in heads:
            xq = q_ref[rows, cols[h]].astype(F32)
            qs.append(xq * _sigmoid(xq))
            th = jnp.tanh(0.5 * f_ref[rows, cols[h]].astype(F32))
            logf = jnp.log(f_mid[h] + f_amp[h] * th)
            kk.append(f_amp[h] - f_amp[h] * th)
            vb.append(i_ref[rows, cols[h]])
            hi = logf.astype(BF16)
            r1 = logf - hi.astype(F32)
            mid = r1.astype(BF16)
            lo = (r1 - mid.astype(F32)).astype(BF16)
            b.append(_dot(tri, hi) + _dot(tri, mid) + _dot(tri, lo))

        qe, b_last, res, upd = [], [], [], []
        for h in heads:
            first = jnp.concatenate([b[h][k * sub:k * sub + 1, :] for k in range(n_all)], axis=0)
            last = jnp.concatenate([b[h][k * sub + sub - 1:(k + 1) * sub, :] for k in range(n_all)], axis=0)
            qt = qs[h] * jnp.exp(b[h] - _rows_to_blocks(first, sub))
            kt = kk[h] * jnp.exp(_rows_to_blocks(last, sub) - b[h])
            bl = [last[(c + 1) * nblk - 1:(c + 1) * nblk, :] for c in range(n_chunks)]
            to_end = jnp.concatenate([bl[c] - last[c * nblk:(c + 1) * nblk, :] for c in range(n_chunks)], axis=0)
            qe.append((qt * _rows_to_blocks(jnp.exp(first), sub)).astype(BF16))
            kd = (kt * _rows_to_blocks(jnp.exp(to_end), sub)).astype(BF16)
            qtb = qt.astype(BF16)
            b_last.append(bl)
            res_h, upd_h = [], []
            for c in range(n_chunks):
                lo_r, hi_r = c * chunk, (c + 1) * chunk
                fr = first[c * nblk:(c + 1) * nblk, :]
                la = last[c * nblk:(c + 1) * nblk, :]
                g = jnp.where(pair_ok,
                              jnp.exp(_rows_to_blocks(fr, nblk) - jnp.concatenate([la] * nblk, axis=0)), 0.0)
                ktc = kt[lo_r:hi_r, :]
                kh = jnp.concatenate(
                    [ktc[jb * sub:(jb + 1) * sub, :] * g[ib * nblk + jb:ib * nblk + jb + 1, :]
                     for ib in range(nblk) for jb in range(nblk)], axis=0).astype(BF16)
                res_h.append(lax.dot_general(qtb[lo_r:hi_r, :], kh, NT_DIMS, preferred_element_type=F32))
                upd_h.append(lax.dot_general(vb[h][lo_r:hi_r, :], kd[lo_r:hi_r, :], TN_DIMS,
                                             preferred_element_type=F32))
            res.append(res_h)
            upd.append(upd_h)

        intra = [[_dot(jnp.where(keep, res[h][c], 0.0).astype(BF16),
                       jnp.concatenate([vb[h][c * chunk:(c + 1) * chunk, :]] * nblk, axis=0))
                  for c in range(n_chunks)] for h in heads]

        st = [st_ref[h] for h in heads]
        for c in range(n_chunks):
            lo_r, hi_r = c * chunk, (c + 1) * chunk
            out_rows = pl.ds(pl.multiple_of(sidx * span + lo_r, chunk), chunk)
            for h in heads:
                o = intra[h][c] + lax.dot_general(qe[h][lo_r:hi_r, :], st[h].astype(BF16), NT_DIMS,
                                                  preferred_element_type=F32)
                st[h] = st[h] * jnp.exp(b_last[h][c]) + upd[h][c]
                ms = jnp.mean(o * o, axis=-1, keepdims=True)
                on = o * lax.rsqrt(ms + RMS_EPS) * gain_ref[h]
                og = og_ref[out_rows, cols[h]].astype(F32)
                o_ref[out_rows, cols[h]] = (on * _sigmoid(og)).astype(o_ref.dtype)
        for h in heads:
            st_ref[h] = st[h]
        return carry

    lax.fori_loop(0, tt // span, one_span, 0)


def _hgrn_call(z, lb, gain, *, bsz, seq, heads, q_col, f_col, i_col, og_col, chunk, sub, unroll):
    staged = sub > 1
    hp = math.gcd(heads, HGRN_HEADS_PER_STEP) if staged else 1
    tt = _tile(seq, 1024 if staged else 256)
    nt = seq // tt
    if staged:
        kern = functools.partial(_hgrn_staged_kernel, tt=tt, span=_tile(tt, HGRN_SPAN), chunk=chunk, sub=sub,
                                 hp=hp)
        head_spec = pl.BlockSpec((hp, 1, HEAD_DIM), lambda b, h, t: (h, 0, 0))
        state = pltpu.VMEM((hp, HEAD_DIM, HEAD_DIM), F32)
    else:
        kern = functools.partial(_hgrn_kernel, tt=tt, chunk=chunk, sub=sub, unroll=unroll)
        head_spec = pl.BlockSpec((None, 1, HEAD_DIM), lambda b, h, t: (h, 0, 0))
        state = pltpu.VMEM((HEAD_DIM, HEAD_DIM), F32)
    width = hp * HEAD_DIM
    assert all(col % hp == 0 for col in (q_col, f_col, i_col, og_col))

    def col_spec(col):
        return pl.BlockSpec((tt, width), lambda b, h, t: (b * nt + t, col // hp + h))

    return pl.pallas_call(
        kern,
        grid=(bsz, heads // hp, nt),
        in_specs=[col_spec(q_col), col_spec(f_col), col_spec(i_col), col_spec(og_col), head_spec, head_spec],
        out_specs=pl.BlockSpec((tt, width), lambda b, h, t: (b * nt + t, h)),
        out_shape=jax.ShapeDtypeStruct((bsz * seq, heads * HEAD_DIM), BF16),
        scratch_shapes=[state],
        compiler_params=_params("parallel", "parallel", "arbitrary"),
        name=f"hgrn_c{chunk}",
    )(z, z, z, z, lb, gain)


def _hgrn(z, lb, gain, **kw):
    fast = functools.partial(_hgrn_call, chunk=HGRN_CHUNK, sub=HGRN_SUB, unroll=True, **kw)
    safe = functools.partial(_hgrn_call, chunk=HGRN_SAFE_CHUNK, sub=1, unroll=False, **kw)
    fast_ok = -(HGRN_SUB - 1) * jnp.log(jnp.min(lb)) <= HGRN_MAX_EXPONENT
    return lax.cond(fast_ok, fast, safe, z, lb, gain)


def _mixout_kernel(*refs, alpha, n_gate, gw):
    oa_ref, ob_ref = refs[0], refs[1]
    ga_refs = refs[2:2 + n_gate]
    gb_refs = refs[2 + n_gate:2 + 2 * n_gate]
    x_ref, wpa_ref, wpb_ref, wo_ref, g_ref, b_ref, o_ref = refs[2 + 2 * n_gate:]
    oa = oa_ref[...]
    ob = ob_ref[...]
    mix = None
    for c in range(n_gate):
        ya = _dot(oa, wpa_ref[:, c * gw:(c + 1) * gw])
        yb = _dot(ob, wpb_ref[:, c * gw:(c + 1) * gw])
        merged = (jax.nn.sigmoid(ga_refs[c][...].astype(F32)) * ya
                  + jax.nn.sigmoid(gb_refs[c][...].astype(F32)) * yb)
        part = _dot(merged.astype(BF16), wo_ref[c * gw:(c + 1) * gw, :])
        mix = part if mix is None else mix + part
    o_ref[...] = _layer_norm(alpha * x_ref[...] + mix, g_ref[...], b_ref[...])


def _mixout(o_a, o_b, z, x, wpa, wpb, wo, ln_g, ln_b, *, alpha, ga_off, gb_off):
    m, d = x.shape
    gw = math.gcd(math.gcd(ga_off, gb_off), d)
    n_gate = d // gw
    tm = _tile(m, 256)
    kern = functools.partial(_mixout_kernel, alpha=alpha, n_gate=n_gate, gw=gw)

    def gate_spec(off, c):
        return pl.BlockSpec((tm, gw), lambda i: (i, off // gw + c))

    def whole(a):
        return pl.BlockSpec(a.shape, lambda i: (0,) * a.ndim, pipeline_mode=pl.Buffered(1))

    def rows(a):
        return pl.BlockSpec((tm, a.shape[1]), lambda i: (i, 0))

    return pl.pallas_call(
        kern,
        grid=(m // tm,),
        in_specs=([rows(o_a), rows(o_b)]
                  + [gate_spec(ga_off, c) for c in range(n_gate)]
                  + [gate_spec(gb_off, c) for c in range(n_gate)]
                  + [rows(x), whole(wpa), whole(wpb), whole(wo), whole(ln_g), whole(ln_b)]),
        out_specs=pl.BlockSpec((tm, d), lambda i: (i, 0)),
        out_shape=jax.ShapeDtypeStruct((m, d), F32),
        compiler_params=_params("parallel"),
        name="mixout",
    )(o_a, o_b, *([z] * (2 * n_gate)), x, wpa, wpb, wo, ln_g, ln_b)


def _ffn_kernel(h_ref, wg_ref, wu_ref, wd_ref, g_ref, b_ref, o_ref, hb_ref, *, alpha, nf):
    f = pl.program_id(1)

    @pl.when(f == 0)
    def _():
        hb_ref[...] = h_ref[...].astype(BF16)
        o_ref[...] = jnp.zeros_like(o_ref)

    hb = hb_ref[...]
    gate = _dot(hb, wg_ref[...])
    up = _dot(hb, wu_ref[...])
    act = (gate * jax.nn.sigmoid(gate) * up).astype(BF16)
    o_ref[...] += _dot(act, wd_ref[...])

    @pl.when(f == nf - 1)
    def _():
        o_ref[...] = _layer_norm(alpha * h_ref[...] + o_ref[...], g_ref[...], b_ref[...])


def _ffn(h, wg, wu, wd, ln_g, ln_b, *, alpha):
    m, d = h.shape
    hidden = wg.shape[1]
    tm, tf = _tile(m, 512), _tile(hidden, 512)
    nf = hidden // tf
    kern = functools.partial(_ffn_kernel, alpha=alpha, nf=nf)
    vec = pl.BlockSpec((1, d), lambda i, f: (0, 0))
    return pl.pallas_call(
        kern,
        grid=(m // tm, nf),
        in_specs=[pl.BlockSpec((tm, d), lambda i, f: (i, 0)),
                  pl.BlockSpec((d, tf), lambda i, f: (0, f)),
                  pl.BlockSpec((d, tf), lambda i, f: (0, f)),
                  pl.BlockSpec((tf, d), lambda i, f: (f, 0)),
                  vec, vec],
        out_specs=pl.BlockSpec((tm, d), lambda i, f: (i, 0)),
        out_shape=jax.ShapeDtypeStruct((m, d), F32),
        scratch_shapes=[pltpu.VMEM((tm, d), BF16)],
        compiler_params=_params("parallel", "arbitrary"),
        name="ffn",
    )(h, wg, wu, wd, ln_g, ln_b)


def kernel(x, w_in, w_proj_a, w_proj_b, w_out, hgrn_norm_g, hgrn_lb_logits, ln1_g, ln1_b,
           w_gate_ffn, w_up_ffn, w_down_ffn, ln2_g, ln2_b):
    bsz, seq, d = x.shape
    depth = w_in.shape[0]
    a_width = w_proj_a.shape[1]
    b_vwidth = w_proj_b.shape[1]
    b_fwidth = hgrn_lb_logits.shape[1]
    assert a_width % HEAD_DIM == 0 and b_vwidth == b_fwidth and b_fwidth % HEAD_DIM == 0
    assert w_in.shape[2] == 3 * a_width + 2 * b_fwidth + 2 * b_vwidth + 2 * d
    a_heads = a_width // HEAD_DIM
    b_heads = b_fwidth // HEAD_DIM
    qa_col, ka_col, va_col = 0, a_heads, 2 * a_heads
    qb_col = 3 * a_heads
    fb_col, ib_col, ogb_col = qb_col + b_heads, qb_col + 2 * b_heads, qb_col + 3 * b_heads
    ga_off = 3 * a_width + 2 * b_fwidth + 2 * b_vwidth
    gb_off = ga_off + d
    alpha = (2.0 * depth) ** 0.25

    lb_all = jnp.cumsum(jax.nn.softmax(hgrn_lb_logits.astype(F32), axis=0), axis=0)

    h = x.reshape(bsz * seq, d)
    for l in range(depth):
        z = _inproj(h, w_in[l].astype(BF16))
        o_a = _moba(z, bsz, seq, a_heads, qa_col, ka_col, va_col)
        o_b = _hgrn(z, lb_all[l].reshape(b_heads, 1, HEAD_DIM),
                    hgrn_norm_g[l].astype(F32).reshape(b_heads, 1, HEAD_DIM),
                    bsz=bsz, seq=seq, heads=b_heads,
                    q_col=qb_col, f_col=fb_col, i_col=ib_col, og_col=ogb_col)
        h = _mixout(o_a, o_b, z, h, w_proj_a[l].astype(BF16), w_proj_b[l].astype(BF16),
                    w_out[l].astype(BF16), ln1_g[l].reshape(1, d), ln1_b[l].reshape(1, d),
                    alpha=alpha, ga_off=ga_off, gb_off=gb_off)
        h = _ffn(h, w_gate_ffn[l].astype(BF16), w_up_ffn[l].astype(BF16), w_down_ffn[l].astype(BF16),
                 ln2_g[l].reshape(1, d), ln2_b[l].reshape(1, d), alpha=alpha)
    return h.reshape(bsz, seq, d)
```

```python
import functools
import math

import jax
import jax.numpy as jnp
from jax import lax
from jax.experimental import pallas as pl
from jax.experimental.pallas import tpu as pltpu

F32 = jnp.float32
BF16 = jnp.bfloat16

HEAD_DIM = 128
MOBA_BLOCK = 256
MOBA_TOPK = 3
MOBA_GROUP = 2
MOBA_HEADS_PER_STEP = 2
MOBA_ONES_ROWS = 16
LOG2_E = 1.4426950408889634
LN_EPS = 1e-5
RMS_EPS = 1e-6
NEG_INF = float("-inf")

HGRN_CHUNK = 64
HGRN_SUB = 16
HGRN_SPAN = 256
HGRN_HEADS_PER_STEP = 4
HGRN_SAFE_CHUNK = 8
HGRN_MAX_EXPONENT = 60.0

VMEM_LIMIT_BYTES = 56 * 1024 * 1024

NT_DIMS = (((1,), (1,)), ((), ()))
TN_DIMS = (((0,), (0,)), ((), ()))


def _tile(n, pref):
    if n <= pref:
        return n
    t = pref - pref % 128
    while t >= 128:
        if n % t == 0:
            return t
        t -= 128
    raise ValueError(f"no 128-aligned tile for {n}")


def _params(*sem):
    return pltpu.CompilerParams(dimension_semantics=sem, vmem_limit_bytes=VMEM_LIMIT_BYTES)


def _dot(a, b):
    return jnp.dot(a, b, preferred_element_type=F32)


def _layer_norm(pre, g, b):
    mu = jnp.mean(pre, axis=-1, keepdims=True)
    d = pre - mu
    var = jnp.mean(d * d, axis=-1, keepdims=True)
    return d * lax.rsqrt(var + LN_EPS) * g + b


def _inproj_kernel(x_ref, w_ref, o_ref, xb_ref):
    @pl.when(pl.program_id(1) == 0)
    def _():
        xb_ref[...] = x_ref[...].astype(BF16)

    o_ref[...] = _dot(xb_ref[...], w_ref[...]).astype(o_ref.dtype)


def _inproj(x, w):
    m, k = x.shape
    n = w.shape[1]
    tm, tn = _tile(m, 1024), _tile(n, 1024)
    return pl.pallas_call(
        _inproj_kernel,
        grid=(m // tm, n // tn),
        in_specs=[pl.BlockSpec((tm, k), lambda i, j: (i, 0)),
                  pl.BlockSpec((k, tn), lambda i, j: (0, j))],
        out_specs=pl.BlockSpec((tm, tn), lambda i, j: (i, j)),
        out_shape=jax.ShapeDtypeStruct((m, n), BF16),
        scratch_shapes=[pltpu.VMEM((tm, k), BF16)],
        compiler_params=_params("parallel", "arbitrary"),
        name="inproj",
    )(x, w)


def _moba_kernel(z_hbm, oa_ref, ob_ref, q_ref, k_ref, v_ref, dma_sem, kmean_ref, vt_ref, qts_ref, sel_ref,
                 sa_ref, sb_ref, ma_ref, mb_ref, pa_ref, pb_ref, acc_ref, cbias_ref,
                 *, nb, nbp, log2_scale, hp, q_col, k_col, v_col):
    i = pl.program_id(2)
    blk = MOBA_BLOCK
    grp = MOBA_GROUP
    heads = range(hp)
    cols = [slice(h * HEAD_DIM, (h + 1) * HEAD_DIM) for h in heads]

    @pl.when(i == 0)
    def _():
        width = hp * HEAD_DIM
        copies = [
            pltpu.make_async_copy(
                z_hbm.at[pl.program_id(0), :, pl.ds((col // hp + pl.program_id(1)) * width, width)], dst,
                dma_sem.at[idx])
            for idx, (col, dst) in enumerate(((q_col, q_ref), (k_col, k_ref), (v_col, v_ref)))]
        for cp in copies:
            cp.start()
        for cp in copies:
            cp.wait()
        key_row = lax.broadcasted_iota(jnp.int32, (blk, blk), 0)
        query_col = lax.broadcasted_iota(jnp.int32, (blk, blk), 1)
        cbias_ref[0] = jnp.zeros((blk, blk), F32)
        cbias_ref[1] = jnp.where(key_row <= query_col, 0.0, NEG_INF)
        kmean_ref[...] = jnp.zeros_like(kmean_ref)

        def prep(n, carry):
            rows = pl.ds(pl.multiple_of(n * blk, blk), blk)
            for h in heads:
                kf = k_ref[rows, cols[h]].astype(F32)
                kmean_ref[h, pl.ds(n, 1), :] = jnp.sum(kf, axis=0, keepdims=True) * (1.0 / blk)
                vt_ref[h, n, :HEAD_DIM, :] = v_ref[rows, cols[h]].astype(F32).T.astype(BF16)
                vt_ref[h, n, HEAD_DIM:, :] = jnp.ones((MOBA_ONES_ROWS, blk), BF16)
            return carry

        lax.fori_loop(0, nb, prep, 0)
        row = lax.broadcasted_iota(jnp.int32, (nbp, blk), 0)

        def select(n, carry):
            rows = pl.ds(pl.multiple_of(n * blk, blk), blk)
            for h in heads:
                qt32 = q_ref[rows, cols[h]].astype(F32).T
                qt = qt32.astype(BF16)
                qts_ref[h, n] = (qt32 * log2_scale).astype(BF16)
                km = kmean_ref[h]
                km_hi = km.astype(BF16)
                km_lo = (km - km_hi.astype(F32)).astype(BF16)
                gate = _dot(km_hi, qt) + _dot(km_lo, qt)
                g = jnp.where(row < n, gate, NEG_INF)
                sel = jnp.zeros((nbp, blk), dtype=jnp.bool_)
                for _ in range(MOBA_TOPK):
                    mx = jnp.max(g, axis=0, keepdims=True)
                    idx = jnp.min(jnp.where(g == mx, row, nbp), axis=0, keepdims=True)
                    pick = (row == idx) & (mx > NEG_INF)
                    sel = sel | pick
                    g = jnp.where(pick, NEG_INF, g)
                sel_ref[h, n] = sel.astype(F32)
            return carry

        lax.fori_loop(0, nb, select, 0, unroll=2)

    blk_a, blk_b = i, nb - 1 - i
    n_a = (blk_a + grp) // grp
    n_total = nb // grp + 1

    def group(g):
        in_b = g >= n_a
        qb = jnp.where(in_b, blk_b, blk_a)
        n = g - jnp.where(in_b, n_a, 0)
        ids = [qb - n * grp - c for c in range(grp)]
        return dict(qb=qb, slot=in_b.astype(jnp.int32), first=(n == 0),
                    js=[jnp.maximum(j, 0) for j in ids], oks=[(j >= 0).astype(F32) for j in ids])

    def scores_into(sbuf, mxbuf, d):
        own = d["first"].astype(jnp.int32)
        for h in heads:
            for c, j in enumerate(d["js"]):
                s = _dot(k_ref[pl.ds(pl.multiple_of(j * blk, blk), blk), cols[h]], qts_ref[h, d["qb"]])
                if c == 0:
                    s = s + cbias_ref[own]
                sbuf[h, c] = s
                mxbuf[h, c] = jnp.broadcast_to(jnp.max(s, axis=0, keepdims=True), (8, blk))

    def softmax_into(pbuf, sbuf, mxbuf, d, m_old):
        first = jnp.full((1, blk), d["first"].astype(F32)) > 0.0
        m_out, alpha_out = [], []
        for h in heads:
            chosen = [sel_ref[h, d["qb"], pl.ds(d["js"][c], 1), :] * d["oks"][c] > 0.0 for c in range(grp)]
            chosen[0] = chosen[0] | first
            m_prev = jnp.where(first, NEG_INF, m_old[h])
            m_new = m_prev
            for c in range(grp):
                m_new = jnp.maximum(m_new, jnp.where(chosen[c], mxbuf[h, c, 0:1, :], NEG_INF))
            for c in range(grp):
                pbuf[h, c] = jnp.exp2(sbuf[h, c] - jnp.where(chosen[c], m_new, jnp.inf)).astype(BF16)
            m_out.append(m_new)
            alpha_out.append(jnp.exp2(m_prev - m_new))
        return m_out, alpha_out

    def accumulate(pbuf, d, alpha):
        for h in heads:
            pv = None
            for c, j in enumerate(d["js"]):
                part = _dot(vt_ref[h, j], pbuf[h, c])
                pv = part if pv is None else pv + part
            acc_ref[d["slot"], h] = alpha[h] * acc_ref[d["slot"], h] + pv

    s_bufs, mx_bufs, p_bufs = (sa_ref, sb_ref), (ma_ref, mb_ref), (pa_ref, pb_ref)
    acc_ref[...] = jnp.zeros_like(acc_ref)
    groups = [group(g) for g in range(n_total)]
    m = [jnp.zeros((1, blk), F32) for _ in heads]
    alpha = None
    scores_into(s_bufs[0], mx_bufs[0], groups[0])
    for g in range(n_total):
        if g + 1 < n_total:
            scores_into(s_bufs[(g + 1) % 2], mx_bufs[(g + 1) % 2], groups[g + 1])
        if g > 0:
            accumulate(p_bufs[(g - 1) % 2], groups[g - 1], alpha)
        m, alpha = softmax_into(p_bufs[g % 2], s_bufs[g % 2], mx_bufs[g % 2], groups[g], m)
    accumulate(p_bufs[(n_total - 1) % 2], groups[n_total - 1], alpha)
    for slot, out_ref in enumerate((oa_ref, ob_ref)):
        for h in heads:
            num = acc_ref[slot, h, :HEAD_DIM, :]
            den = acc_ref[slot, h, HEAD_DIM:HEAD_DIM + 1, :]
            out_ref[cols[h], :] = (num / den).astype(out_ref.dtype)


def _moba(z, bsz, seq, heads, q_col, k_col, v_col):
    blk = MOBA_BLOCK
    assert seq % (2 * blk) == 0 and MOBA_GROUP == 2
    nb = seq // blk
    half = nb // 2
    nbp = -(-nb // 8) * 8
    acc_rows = HEAD_DIM + MOBA_ONES_ROWS
    hp = math.gcd(heads, MOBA_HEADS_PER_STEP)
    assert all(col % hp == 0 for col in (q_col, k_col, v_col))
    width = hp * HEAD_DIM
    z3 = z.reshape(bsz, seq, z.shape[1])
    kern = functools.partial(_moba_kernel, nb=nb, nbp=nbp, log2_scale=HEAD_DIM ** -0.5 * LOG2_E, hp=hp,
                             q_col=q_col, k_col=k_col, v_col=v_col)
    seq_buf = pltpu.VMEM((seq, width), BF16)
    return pl.pallas_call(
        kern,
        grid=(bsz, heads // hp, half),
        in_specs=[pl.BlockSpec(memory_space=pl.ANY)],
        out_specs=[pl.BlockSpec((width, blk), lambda b, h, i: (h, b * half + i))] * 2,
        out_shape=[jax.ShapeDtypeStruct((heads * HEAD_DIM, bsz * seq // 2), BF16)] * 2,
        scratch_shapes=[seq_buf, seq_buf, seq_buf,
                        pltpu.SemaphoreType.DMA((3,)),
                        pltpu.VMEM((hp, nbp, HEAD_DIM), F32),
                        pltpu.VMEM((hp, nb, acc_rows, blk), BF16),
                        pltpu.VMEM((hp, nb, HEAD_DIM, blk), BF16),
                        pltpu.VMEM((hp, nb, nbp, blk), F32),
                        pltpu.VMEM((hp, MOBA_GROUP, blk, blk), F32),
                        pltpu.VMEM((hp, MOBA_GROUP, blk, blk), F32),
                        pltpu.VMEM((hp, MOBA_GROUP, 8, blk), F32),
                        pltpu.VMEM((hp, MOBA_GROUP, 8, blk), F32),
                        pltpu.VMEM((hp, MOBA_GROUP, blk, blk), BF16),
                        pltpu.VMEM((hp, MOBA_GROUP, blk, blk), BF16),
                        pltpu.VMEM((2, hp, acc_rows, blk), F32),
                        pltpu.VMEM((2, blk, blk), F32)],
        compiler_params=_params("parallel", "parallel", "arbitrary"),
        name="moba",
    )(z3)


def _hgrn_kernel(q_ref, f_ref, i_ref, og_ref, lb_ref, gain_ref, o_ref, st_ref, *, tt, chunk, sub, unroll):
    @pl.when(pl.program_id(2) == 0)
    def _():
        st_ref[...] = jnp.zeros_like(st_ref)

    n_blocks = chunk // sub
    stack = n_blocks * chunk
    lb = lb_ref[...]
    gain = gain_ref[...]
    tri = (lax.broadcasted_iota(jnp.int32, (chunk, chunk), 0)
           >= lax.broadcasted_iota(jnp.int32, (chunk, chunk), 1)).astype(BF16)
    r_i = lax.broadcasted_iota(jnp.int32, (chunk, stack), 0)
    c_i = lax.broadcasted_iota(jnp.int32, (chunk, stack), 1)
    keep = ((c_i // chunk) == (r_i // sub)) & ((c_i % chunk) <= r_i)
    row_c = lax.broadcasted_iota(jnp.int32, (chunk, HEAD_DIM), 0)

    def one_chunk(n, carry):
        off = pl.multiple_of(n * chunk, chunk)
        xq = q_ref[pl.ds(off, chunk), :].astype(F32)
        qs = xq * jax.nn.sigmoid(xq)
        f = lb + (1.0 - lb) * jax.nn.sigmoid(f_ref[pl.ds(off, chunk), :].astype(F32))
        logf = jnp.log(f)
        kk = 1.0 - f
        vb = i_ref[pl.ds(off, chunk), :]

        hi = logf.astype(BF16)
        r1 = logf - hi.astype(F32)
        mid = r1.astype(BF16)
        lo = (r1 - mid.astype(F32)).astype(BF16)
        b = _dot(tri, hi) + _dot(tri, mid) + _dot(tri, lo)
        b_last = b[chunk - 1:chunk, :]

        st = st_ref[...]
        qe = (qs * jnp.exp(b)).astype(BF16)
        o_inter = lax.dot_general(qe, st.astype(BF16), NT_DIMS, preferred_element_type=F32)

        ref_rows = jnp.concatenate(
            [jnp.broadcast_to(b[blk * sub:blk * sub + 1, :], (sub, HEAD_DIM)) for blk in range(n_blocks)],
            axis=0)
        qt = (qs * jnp.exp(b - ref_rows)).astype(BF16)
        kh = jnp.concatenate(
            [(kk * jnp.exp(jnp.where(row_c < (blk + 1) * sub, b[blk * sub:blk * sub + 1, :] - b, NEG_INF))
              ).astype(BF16) for blk in range(n_blocks)], axis=0)
        res = lax.dot_general(qt, kh, NT_DIMS, preferred_element_type=F32)
        a = jnp.where(keep, res, 0.0).astype(BF16)
        o = o_inter + _dot(a, jnp.concatenate([vb] * n_blocks, axis=0))

        kd = (kk * jnp.exp(b_last - b)).astype(BF16)
        st_ref[...] = st * jnp.exp(b_last) + lax.dot_general(vb, kd, TN_DIMS, preferred_element_type=F32)

        ms = jnp.mean(o * o, axis=-1, keepdims=True)
        on = o * lax.rsqrt(ms + RMS_EPS) * gain
        og = og_ref[pl.ds(off, chunk), :].astype(F32)
        o_ref[pl.ds(off, chunk), :] = (on * jax.nn.sigmoid(og)).astype(o_ref.dtype)
        return carry

    lax.fori_loop(0, tt // chunk, one_chunk, 0, unroll=unroll)


def _sigmoid(x):
    return 0.5 * jnp.tanh(0.5 * x) + 0.5


def _rows_to_blocks(rows, sub):
    return jnp.concatenate([jnp.broadcast_to(rows[k:k + 1, :], (sub, HEAD_DIM)) for k in range(rows.shape[0])],
                           axis=0)


def _hgrn_staged_kernel(q_ref, f_ref, i_ref, og_ref, lb_ref, gain_ref, o_ref, st_ref, *, tt, span, chunk, sub, hp):
    @pl.when(pl.program_id(2) == 0)
    def _():
        st_ref[...] = jnp.zeros_like(st_ref)

    n_chunks = span // chunk
    nblk = chunk // sub
    n_all = span // sub
    stack = nblk * chunk
    r_s = lax.broadcasted_iota(jnp.int32, (span, span), 0)
    c_s = lax.broadcasted_iota(jnp.int32, (span, span), 1)
    tri = ((r_s >= c_s) & (r_s // chunk == c_s // chunk)).astype(BF16)
    r_i = lax.broadcasted_iota(jnp.int32, (chunk, stack), 0)
    c_i = lax.broadcasted_iota(jnp.int32, (chunk, stack), 1)
    keep = ((c_i // chunk) == (r_i // sub)) & ((c_i % chunk) <= r_i)
    pair = lax.broadcasted_iota(jnp.int32, (nblk * nblk, HEAD_DIM), 0)
    pair_ok = (pair % nblk) <= (pair // nblk)
    heads = range(hp)
    cols = [slice(h * HEAD_DIM, (h + 1) * HEAD_DIM) for h in heads]
    f_mid = [0.5 + 0.5 * lb_ref[h] for h in heads]
    f_amp = [0.5 - 0.5 * lb_ref[h] for h in heads]

    def one_span(sidx, carry):
        rows = pl.ds(pl.multiple_of(sidx * span, span), span)

        qs, kk, vb, b = [], [], [], []
        for h in heads:
            xq = q_ref[rows, cols[h]].astype(F32)
            qs.append(xq * _sigmoid(xq))
            th = jnp.tanh(0.5 * f_ref[rows, cols[h]].astype(F32))
            logf = jnp.log(f_mid[h] + f_amp[h] * th)
            kk.append(f_amp[h] - f_amp[h] * th)
            vb.append(i_ref[rows, cols[h]])
            hi = logf.astype(BF16)
            r1 = logf - hi.astype(F32)
            mid = r1.astype(BF16)
            lo = (r1 - mid.astype(F32)).astype(BF16)
            b.append(_dot(tri, hi) + _dot(tri, mid) + _dot(tri, lo))

        qe, b_last, res, upd = [], [], [], []
        for h in heads:
            first = jnp.concatenate([b[h][k * sub:k * sub + 1, :] for k in range(n_all)], axis=0)
            last = jnp.concatenate([b[h][k * sub + sub - 1:(k + 1) * sub, :] for k in range(n_all)], axis=0)
            qt = qs[h] * jnp.exp(b[h] - _rows_to_blocks(first, sub))
            kt = kk[h] * jnp.exp(_rows_to_blocks(last, sub) - b[h])
            bl = [last[(c + 1) * nblk - 1:(c + 1) * nblk, :] for c in range(n_chunks)]
            to_end = jnp.concatenate([bl[c] - last[c * nblk:(c + 1) * nblk, :] for c in range(n_chunks)], axis=0)
            qe.append((qt * _rows_to_blocks(jnp.exp(first), sub)).astype(BF16))
            kd = (kt * _rows_to_blocks(jnp.exp(to_end), sub)).astype(BF16)
            qtb = qt.astype(BF16)
            b_last.append(bl)
            res_h, upd_h = [], []
            for c in range(n_chunks):
                lo_r, hi_r = c * chunk, (c + 1) * chunk
                fr = first[c * nblk:(c + 1) * nblk, :]
                la = last[c * nblk:(c + 1) * nblk, :]
                g = jnp.where(pair_ok,
                              jnp.exp(_rows_to_blocks(fr, nblk) - jnp.concatenate([la] * nblk, axis=0)), 0.0)
                ktc = kt[lo_r:hi_r, :]
                kh = jnp.concatenate(
                    [ktc[jb * sub:(jb + 1) * sub, :] * g[ib * nblk + jb:ib * nblk + jb + 1, :]
                     for ib in range(nblk) for jb in range(nblk)], axis=0).astype(BF16)
                res_h.append(lax.dot_general(qtb[lo_r:hi_r, :], kh, NT_DIMS, preferred_element_type=F32))
                upd_h.append(lax.dot_general(vb[h][lo_r:hi_r, :], kd[lo_r:hi_r, :], TN_DIMS,
                                             preferred_element_type=F32))
            res.append(res_h)
            upd.append(upd_h)

        intra = [[_dot(jnp.where(keep, res[h][c], 0.0).astype(BF16),
                       jnp.concatenate([vb[h][c * chunk:(c + 1) * chunk, :]] * nblk, axis=0))
                  for c in range(n_chunks)] for h in heads]

        st = [st_ref[h] for h in heads]
        for c in range(n_chunks):
            lo_r, hi_r = c * chunk, (c + 1) * chunk
            out_rows = pl.ds(pl.multiple_of(sidx * span + lo_r, chunk), chunk)
            for h in heads:
                o = intra[h][c] + lax.dot_general(qe[h][lo_r:hi_r, :], st[h].astype(BF16), NT_DIMS,
                                                  preferred_element_type=F32)
                st[h] = st[h] * jnp.exp(b_last[h][c]) + upd[h][c]
                ms = jnp.mean(o * o, axis=-1, keepdims=True)
                on = o * lax.rsqrt(ms + RMS_EPS) * gain_ref[h]
                og = og_ref[out_rows, cols[h]].astype(F32)
                o_ref[out_rows, cols[h]] = (on * _sigmoid(og)).astype(o_ref.dtype)
        for h in heads:
            st_ref[h] = st[h]
        return carry

    lax.fori_loop(0, tt // span, one_span, 0)


def _hgrn_call(z, lb, gain, *, bsz, seq, heads, q_col, f_col, i_col, og_col, chunk, sub, unroll):
    staged = sub > 1
    hp = math.gcd(heads, HGRN_HEADS_PER_STEP) if staged else 1
    tt = _tile(seq, 1024 if staged else 256)
    nt = seq // tt
    if staged:
        kern = functools.partial(_hgrn_staged_kernel, tt=tt, span=_tile(tt, HGRN_SPAN), chunk=chunk, sub=sub,
                                 hp=hp)
        head_spec = pl.BlockSpec((hp, 1, HEAD_DIM), lambda b, h, t: (h, 0, 0))
        state = pltpu.VMEM((hp, HEAD_DIM, HEAD_DIM), F32)
    else:
        kern = functools.partial(_hgrn_kernel, tt=tt, chunk=chunk, sub=sub, unroll=unroll)
        head_spec = pl.BlockSpec((None, 1, HEAD_DIM), lambda b, h, t: (h, 0, 0))
        state = pltpu.VMEM((HEAD_DIM, HEAD_DIM), F32)
    width = hp * HEAD_DIM
    assert all(col % hp == 0 for col in (q_col, f_col, i_col, og_col))

    def col_spec(col):
        return pl.BlockSpec((tt, width), lambda b, h, t: (b * nt + t, col // hp + h))

    return pl.pallas_call(
        kern,
        grid=(bsz, heads // hp, nt),
        in_specs=[col_spec(q_col), col_spec(f_col), col_spec(i_col), col_spec(og_col), head_spec, head_spec],
        out_specs=pl.BlockSpec((tt, width), lambda b, h, t: (b * nt + t, h)),
        out_shape=jax.ShapeDtypeStruct((bsz * seq, heads * HEAD_DIM), BF16),
        scratch_shapes=[state],
        compiler_params=_params("parallel", "parallel", "arbitrary"),
        name=f"hgrn_c{chunk}",
    )(z, z, z, z, lb, gain)


def _hgrn(z, lb, gain, **kw):
    fast = functools.partial(_hgrn_call, chunk=HGRN_CHUNK, sub=HGRN_SUB, unroll=True, **kw)
    safe = functools.partial(_hgrn_call, chunk=HGRN_SAFE_CHUNK, sub=1, unroll=False, **kw)
    fast_ok = -(HGRN_SUB - 1) * jnp.log(jnp.min(lb)) <= HGRN_MAX_EXPONENT
    return lax.cond(fast_ok, fast, safe, z, lb, gain)


def _mixout_kernel(*refs, alpha, n_gate, gw, nb):
    oa_lo_ref, oa_hi_ref, ob_ref = refs[:3]
    ga_refs = refs[3:3 + n_gate]
    gb_refs = refs[3 + n_gate:3 + 2 * n_gate]
    x_ref, wpa_ref, wpb_ref, wo_ref, g_ref, b_ref, o_ref = refs[3 + 2 * n_gate:]
    in_lo = pl.program_id(0) % nb < nb // 2
    oa_t = jnp.where(in_lo, oa_lo_ref[...], oa_hi_ref[...])
    ob = ob_ref[...]
    mix = None
    for c in range(n_gate):
        ya = lax.dot_general(oa_t, wpa_ref[:, c * gw:(c + 1) * gw], TN_DIMS, preferred_element_type=F32)
        yb = _dot(ob, wpb_ref[:, c * gw:(c + 1) * gw])
        merged = (jax.nn.sigmoid(ga_refs[c][...].astype(F32)) * ya
                  + jax.nn.sigmoid(gb_refs[c][...].astype(F32)) * yb)
        part = _dot(merged.astype(BF16), wo_ref[c * gw:(c + 1) * gw, :])
        mix = part if mix is None else mix + part
    o_ref[...] = _layer_norm(alpha * x_ref[...] + mix, g_ref[...], b_ref[...])


def _mixout(o_a_halves, o_b, z, x, wpa, wpb, wo, ln_g, ln_b, *, alpha, ga_off, gb_off, nb):
    m, d = x.shape
    gw = math.gcd(math.gcd(ga_off, gb_off), d)
    n_gate = d // gw
    tm = MOBA_BLOCK
    half = nb // 2
    kern = functools.partial(_mixout_kernel, alpha=alpha, n_gate=n_gate, gw=gw, nb=nb)

    def gate_spec(off, c):
        return pl.BlockSpec((tm, gw), lambda i: (i, off // gw + c))

    def whole(a):
        return pl.BlockSpec(a.shape, lambda i: (0,) * a.ndim, pipeline_mode=pl.Buffered(1))

    def rows(a):
        return pl.BlockSpec((tm, a.shape[1]), lambda i: (i, 0))

    def attn_spec(pos):
        return pl.BlockSpec((o_a_halves[0].shape[0], tm), lambda i: (0, (i // nb) * half + pos(i % nb)))

    return pl.pallas_call(
        kern,
        grid=(m // tm,),
        in_specs=([attn_spec(lambda blk: jnp.minimum(blk, half - 1)),
                   attn_spec(lambda blk: nb - 1 - jnp.maximum(blk, half)), rows(o_b)]
                  + [gate_spec(ga_off, c) for c in range(n_gate)]
                  + [gate_spec(gb_off, c) for c in range(n_gate)]
                  + [rows(x), whole(wpa), whole(wpb), whole(wo), whole(ln_g), whole(ln_b)]),
        out_specs=pl.BlockSpec((tm, d), lambda i: (i, 0)),
        out_shape=jax.ShapeDtypeStruct((m, d), F32),
        compiler_params=_params("parallel"),
        name="mixout",
    )(*o_a_halves, o_b, *([z] * (2 * n_gate)), x, wpa, wpb, wo, ln_g, ln_b)


def _ffn_kernel(h_ref, wg_ref, wu_ref, wd_ref, g_ref, b_ref, o_ref, hb_ref, *, alpha, nf):
    f = pl.program_id(1)

    @pl.when(f == 0)
    def _():
        hb_ref[...] = h_ref[...].astype(BF16)
        o_ref[...] = jnp.zeros_like(o_ref)

    hb = hb_ref[...]
    gate = _dot(hb, wg_ref[...])
    up = _dot(hb, wu_ref[...])
    act = (gate * jax.nn.sigmoid(gate) * up).astype(BF16)
    o_ref[...] += _dot(act, wd_ref[...])

    @pl.when(f == nf - 1)
    def _():
        o_ref[...] = _layer_norm(alpha * h_ref[...] + o_ref[...], g_ref[...], b_ref[...])


def _ffn(h, wg, wu, wd, ln_g, ln_b, *, alpha):
    m, d = h.shape
    hidden = wg.shape[1]
    tm, tf = _tile(m, 512), _tile(hidden, 512)
    nf = hidden // tf
    kern = functools.partial(_ffn_kernel, alpha=alpha, nf=nf)
    vec = pl.BlockSpec((1, d), lambda i, f: (0, 0))
    return pl.pallas_call(
        kern,
        grid=(m // tm, nf),
        in_specs=[pl.BlockSpec((tm, d), lambda i, f: (i, 0)),
                  pl.BlockSpec((d, tf), lambda i, f: (0, f)),
                  pl.BlockSpec((d, tf), lambda i, f: (0, f)),
                  pl.BlockSpec((tf, d), lambda i, f: (f, 0)),
                  vec, vec],
        out_specs=pl.BlockSpec((tm, d), lambda i, f: (i, 0)),
        out_shape=jax.ShapeDtypeStruct((m, d), F32),
        scratch_shapes=[pltpu.VMEM((tm, d), BF16)],
        compiler_params=_params("parallel", "arbitrary"),
        name="ffn",
    )(h, wg, wu, wd, ln_g, ln_b)


def kernel(x, w_in, w_proj_a, w_proj_b, w_out, hgrn_norm_g, hgrn_lb_logits, ln1_g, ln1_b,
           w_gate_ffn, w_up_ffn, w_down_ffn, ln2_g, ln2_b):
    bsz, seq, d = x.shape
    depth = w_in.shape[0]
    a_width = w_proj_a.shape[1]
    b_vwidth = w_proj_b.shape[1]
    b_fwidth = hgrn_lb_logits.shape[1]
    assert a_width % HEAD_DIM == 0 and b_vwidth == b_fwidth and b_fwidth % HEAD_DIM == 0
    assert w_in.shape[2] == 3 * a_width + 2 * b_fwidth + 2 * b_vwidth + 2 * d
    a_heads = a_width // HEAD_DIM
    b_heads = b_fwidth // HEAD_DIM
    qa_col, ka_col, va_col = 0, a_heads, 2 * a_heads
    qb_col = 3 * a_heads
    fb_col, ib_col, ogb_col = qb_col + b_heads, qb_col + 2 * b_heads, qb_col + 3 * b_heads
    ga_off = 3 * a_width + 2 * b_fwidth + 2 * b_vwidth
    gb_off = ga_off + d
    alpha = (2.0 * depth) ** 0.25

    lb_all = jnp.cumsum(jax.nn.softmax(hgrn_lb_logits.astype(F32), axis=0), axis=0)

    h = x.reshape(bsz * seq, d)
    for l in range(depth):
        z = _inproj(h, w_in[l].astype(BF16))
        o_a = _moba(z, bsz, seq, a_heads, qa_col, ka_col, va_col)
        o_b = _hgrn(z, lb_all[l].reshape(b_heads, 1, HEAD_DIM),
                    hgrn_norm_g[l].astype(F32).reshape(b_heads, 1, HEAD_DIM),
                    bsz=bsz, seq=seq, heads=b_heads,
                    q_col=qb_col, f_col=fb_col, i_col=ib_col, og_col=ogb_col)
        h = _mixout(o_a, o_b, z, h, w_proj_a[l].astype(BF16), w_proj_b[l].astype(BF16),
                    w_out[l].astype(BF16), ln1_g[l].reshape(1, d), ln1_b[l].reshape(1, d),
                    alpha=alpha, ga_off=ga_off, gb_off=gb_off, nb=seq // MOBA_BLOCK)
        h = _ffn(h, w_gate_ffn[l].astype(BF16), w_up_ffn[l].astype(BF16), w_down_ffn[l].astype(BF16),
                 ln2_g[l].reshape(1, d), ln2_b[l].reshape(1, d), alpha=alpha)
    return h.reshape(bsz, seq, d)
```

```python
import functools
import math

import jax
import jax.numpy as jnp
from jax import lax
from jax.experimental import pallas as pl
from jax.experimental.pallas import tpu as pltpu

F32 = jnp.float32
BF16 = jnp.bfloat16

HEAD_DIM = 128
MOBA_BLOCK = 256
MOBA_TOPK = 3
MOBA_GROUP = 2
MOBA_HEADS_PER_STEP = 2
MOBA_ONES_ROWS = 16
LOG2_E = 1.4426950408889634
LN_EPS = 1e-5
RMS_EPS = 1e-6
NEG_INF = float("-inf")

HGRN_CHUNK = 64
HGRN_SUB = 16
HGRN_SPAN = 256
HGRN_HEADS_PER_STEP = 4
HGRN_SAFE_CHUNK = 8
HGRN_MAX_EXPONENT = 60.0

VMEM_LIMIT_BYTES = 56 * 1024 * 1024
FFN_ROW_CHUNK = 512

NT_DIMS = (((1,), (1,)), ((), ()))
TN_DIMS = (((0,), (0,)), ((), ()))


def _tile(n, pref):
    if n <= pref:
        return n
    t = pref - pref % 128
    while t >= 128:
        if n % t == 0:
            return t
        t -= 128
    raise ValueError(f"no 128-aligned tile for {n}")


def _params(*sem):
    return pltpu.CompilerParams(dimension_semantics=sem, vmem_limit_bytes=VMEM_LIMIT_BYTES)


def _dot(a, b):
    return jnp.dot(a, b, preferred_element_type=F32)


def _layer_norm(pre, g, b):
    mu = jnp.mean(pre, axis=-1, keepdims=True)
    d = pre - mu
    var = jnp.mean(d * d, axis=-1, keepdims=True)
    return d * lax.rsqrt(var + LN_EPS) * g + b


def _inproj_kernel(x_ref, w_ref, o_ref, xb_ref):
    @pl.when(pl.program_id(1) == 0)
    def _():
        xb_ref[...] = x_ref[...].astype(BF16)

    o_ref[...] = _dot(xb_ref[...], w_ref[...]).astype(o_ref.dtype)


def _inproj(x, w):
    m, k = x.shape
    n = w.shape[1]
    tm, tn = _tile(m, 1024), _tile(n, 1024)
    return pl.pallas_call(
        _inproj_kernel,
        grid=(m // tm, n // tn),
        in_specs=[pl.BlockSpec((tm, k), lambda i, j: (i, 0)),
                  pl.BlockSpec((k, tn), lambda i, j: (0, j))],
        out_specs=pl.BlockSpec((tm, tn), lambda i, j: (i, j)),
        out_shape=jax.ShapeDtypeStruct((m, n), BF16),
        scratch_shapes=[pltpu.VMEM((tm, k), BF16)],
        compiler_params=_params("parallel", "arbitrary"),
        name="inproj",
    )(x, w)


def _moba_kernel(z_hbm, oa_ref, ob_ref, q_ref, k_ref, v_ref, dma_sem, kmean_ref, vt_ref, qts_ref, sel_ref,
                 sa_ref, sb_ref, ma_ref, mb_ref, pa_ref, pb_ref, acc_ref, cbias_ref,
                 *, nb, nbp, log2_scale, hp, q_col, k_col, v_col):
    i = pl.program_id(2)
    blk = MOBA_BLOCK
    grp = MOBA_GROUP
    heads = range(hp)
    cols = [slice(h * HEAD_DIM, (h + 1) * HEAD_DIM) for h in heads]

    @pl.when(i == 0)
    def _():
        width = hp * HEAD_DIM
        copies = [
            pltpu.make_async_copy(
                z_hbm.at[pl.program_id(0), :, pl.ds((col // hp + pl.program_id(1)) * width, width)], dst,
                dma_sem.at[idx])
            for idx, (col, dst) in enumerate(((q_col, q_ref), (k_col, k_ref), (v_col, v_ref)))]
        for cp in copies:
            cp.start()
        for cp in copies:
            cp.wait()
        key_row = lax.broadcasted_iota(jnp.int32, (blk, blk), 0)
        query_col = lax.broadcasted_iota(jnp.int32, (blk, blk), 1)
        cbias_ref[0] = jnp.zeros((blk, blk), F32)
        cbias_ref[1] = jnp.where(key_row <= query_col, 0.0, NEG_INF)
        kmean_ref[...] = jnp.zeros_like(kmean_ref)

        def prep(n, carry):
            rows = pl.ds(pl.multiple_of(n * blk, blk), blk)
            for h in heads:
                kf = k_ref[rows, cols[h]].astype(F32)
                kmean_ref[h, pl.ds(n, 1), :] = jnp.sum(kf, axis=0, keepdims=True) * (1.0 / blk)
                vt_ref[h, n, :HEAD_DIM, :] = v_ref[rows, cols[h]].astype(F32).T.astype(BF16)
                vt_ref[h, n, HEAD_DIM:, :] = jnp.ones((MOBA_ONES_ROWS, blk), BF16)
            return carry

        lax.fori_loop(0, nb, prep, 0)
        row = lax.broadcasted_iota(jnp.int32, (nbp, blk), 0)

        def select(n, carry):
            rows = pl.ds(pl.multiple_of(n * blk, blk), blk)
            for h in heads:
                qt32 = q_ref[rows, cols[h]].astype(F32).T
                qt = qt32.astype(BF16)
                qts_ref[h, n] = (qt32 * log2_scale).astype(BF16)
                km = kmean_ref[h]
                km_hi = km.astype(BF16)
                km_lo = (km - km_hi.astype(F32)).astype(BF16)
                gate = _dot(km_hi, qt) + _dot(km_lo, qt)
                g = jnp.where(row < n, gate, NEG_INF)
                sel = jnp.zeros((nbp, blk), dtype=jnp.bool_)
                for _ in range(MOBA_TOPK):
                    mx = jnp.max(g, axis=0, keepdims=True)
                    idx = jnp.min(jnp.where(g == mx, row, nbp), axis=0, keepdims=True)
                    pick = (row == idx) & (mx > NEG_INF)
                    sel = sel | pick
                    g = jnp.where(pick, NEG_INF, g)
                sel_ref[h, n] = sel.astype(F32)
            return carry

        lax.fori_loop(0, nb, select, 0, unroll=2)

    blk_a, blk_b = i, nb - 1 - i
    n_a = (blk_a + grp) // grp
    n_total = nb // grp + 1

    def group(g):
        in_b = g >= n_a
        qb = jnp.where(in_b, blk_b, blk_a)
        n = g - jnp.where(in_b, n_a, 0)
        ids = [qb - n * grp - c for c in range(grp)]
        return dict(qb=qb, slot=in_b.astype(jnp.int32), first=(n == 0),
                    js=[jnp.maximum(j, 0) for j in ids], oks=[(j >= 0).astype(F32) for j in ids])

    def scores_into(sbuf, mxbuf, d):
        own = d["first"].astype(jnp.int32)
        for h in heads:
            for c, j in enumerate(d["js"]):
                s = _dot(k_ref[pl.ds(pl.multiple_of(j * blk, blk), blk), cols[h]], qts_ref[h, d["qb"]])
                if c == 0:
                    s = s + cbias_ref[own]
                sbuf[h, c] = s
                mxbuf[h, c] = jnp.broadcast_to(jnp.max(s, axis=0, keepdims=True), (8, blk))

    def softmax_into(pbuf, sbuf, mxbuf, d, m_old):
        first = jnp.full((1, blk), d["first"].astype(F32)) > 0.0
        m_out, alpha_out = [], []
        for h in heads:
            chosen = [sel_ref[h, d["qb"], pl.ds(d["js"][c], 1), :] * d["oks"][c] > 0.0 for c in range(grp)]
            chosen[0] = chosen[0] | first
            m_prev = jnp.where(first, NEG_INF, m_old[h])
            m_new = m_prev
            for c in range(grp):
                m_new = jnp.maximum(m_new, jnp.where(chosen[c], mxbuf[h, c, 0:1, :], NEG_INF))
            for c in range(grp):
                pbuf[h, c] = jnp.exp2(sbuf[h, c] - jnp.where(chosen[c], m_new, jnp.inf)).astype(BF16)
            m_out.append(m_new)
            alpha_out.append(jnp.exp2(m_prev - m_new))
        return m_out, alpha_out

    def accumulate(pbuf, d, alpha):
        for h in heads:
            pv = None
            for c, j in enumerate(d["js"]):
                part = _dot(vt_ref[h, j], pbuf[h, c])
                pv = part if pv is None else pv + part
            acc_ref[d["slot"], h] = alpha[h] * acc_ref[d["slot"], h] + pv

    s_bufs, mx_bufs, p_bufs = (sa_ref, sb_ref), (ma_ref, mb_ref), (pa_ref, pb_ref)
    acc_ref[...] = jnp.zeros_like(acc_ref)
    groups = [group(g) for g in range(n_total)]
    m = [jnp.zeros((1, blk), F32) for _ in heads]
    alpha = None
    scores_into(s_bufs[0], mx_bufs[0], groups[0])
    for g in range(n_total):
        if g + 1 < n_total:
            scores_into(s_bufs[(g + 1) % 2], mx_bufs[(g + 1) % 2], groups[g + 1])
        if g > 0:
            accumulate(p_bufs[(g - 1) % 2], groups[g - 1], alpha)
        m, alpha = softmax_into(p_bufs[g % 2], s_bufs[g % 2], mx_bufs[g % 2], groups[g], m)
    accumulate(p_bufs[(n_total - 1) % 2], groups[n_total - 1], alpha)
    for slot, out_ref in enumerate((oa_ref, ob_ref)):
        for h in heads:
            num = acc_ref[slot, h, :HEAD_DIM, :]
            den = acc_ref[slot, h, HEAD_DIM:HEAD_DIM + 1, :]
            out_ref[cols[h], :] = (num / den).astype(out_ref.dtype)


def _moba(z, bsz, seq, heads, q_col, k_col, v_col):
    blk = MOBA_BLOCK
    assert seq % (2 * blk) == 0 and MOBA_GROUP == 2
    nb = seq // blk
    half = nb // 2
    nbp = -(-nb // 8) * 8
    acc_rows = HEAD_DIM + MOBA_ONES_ROWS
    hp = math.gcd(heads, MOBA_HEADS_PER_STEP)
    assert all(col % hp == 0 for col in (q_col, k_col, v_col))
    width = hp * HEAD_DIM
    z3 = z.reshape(bsz, seq, z.shape[1])
    kern = functools.partial(_moba_kernel, nb=nb, nbp=nbp, log2_scale=HEAD_DIM ** -0.5 * LOG2_E, hp=hp,
                             q_col=q_col, k_col=k_col, v_col=v_col)
    seq_buf = pltpu.VMEM((seq, width), BF16)
    return pl.pallas_call(
        kern,
        grid=(bsz, heads // hp, half),
        in_specs=[pl.BlockSpec(memory_space=pl.ANY)],
        out_specs=[pl.BlockSpec((width, blk), lambda b, h, i: (h, b * half + i))] * 2,
        out_shape=[jax.ShapeDtypeStruct((heads * HEAD_DIM, bsz * seq // 2), BF16)] * 2,
        scratch_shapes=[seq_buf, seq_buf, seq_buf,
                        pltpu.SemaphoreType.DMA((3,)),
                        pltpu.VMEM((hp, nbp, HEAD_DIM), F32),
                        pltpu.VMEM((hp, nb, acc_rows, blk), BF16),
                        pltpu.VMEM((hp, nb, HEAD_DIM, blk), BF16),
                        pltpu.VMEM((hp, nb, nbp, blk), F32),
                        pltpu.VMEM((hp, MOBA_GROUP, blk, blk), F32),
                        pltpu.VMEM((hp, MOBA_GROUP, blk, blk), F32),
                        pltpu.VMEM((hp, MOBA_GROUP, 8, blk), F32),
                        pltpu.VMEM((hp, MOBA_GROUP, 8, blk), F32),
                        pltpu.VMEM((hp, MOBA_GROUP, blk, blk), BF16),
                        pltpu.VMEM((hp, MOBA_GROUP, blk, blk), BF16),
                        pltpu.VMEM((2, hp, acc_rows, blk), F32),
                        pltpu.VMEM((2, blk, blk), F32)],
        compiler_params=_params("parallel", "parallel", "arbitrary"),
        name="moba",
    )(z3)


def _hgrn_kernel(q_ref, f_ref, i_ref, og_ref, lb_ref, gain_ref, o_ref, st_ref, *, tt, chunk, sub, unroll):
    @pl.when(pl.program_id(2) == 0)
    def _():
        st_ref[...] = jnp.zeros_like(st_ref)

    n_blocks = chunk // sub
    stack = n_blocks * chunk
    lb = lb_ref[...]
    gain = gain_ref[...]
    tri = (lax.broadcasted_iota(jnp.int32, (chunk, chunk), 0)
           >= lax.broadcasted_iota(jnp.int32, (chunk, chunk), 1)).astype(BF16)
    r_i = lax.broadcasted_iota(jnp.int32, (chunk, stack), 0)
    c_i = lax.broadcasted_iota(jnp.int32, (chunk, stack), 1)
    keep = ((c_i // chunk) == (r_i // sub)) & ((c_i % chunk) <= r_i)
    row_c = lax.broadcasted_iota(jnp.int32, (chunk, HEAD_DIM), 0)

    def one_chunk(n, carry):
        off = pl.multiple_of(n * chunk, chunk)
        xq = q_ref[pl.ds(off, chunk), :].astype(F32)
        qs = xq * jax.nn.sigmoid(xq)
        f = lb + (1.0 - lb) * jax.nn.sigmoid(f_ref[pl.ds(off, chunk), :].astype(F32))
        logf = jnp.log(f)
        kk = 1.0 - f
        vb = i_ref[pl.ds(off, chunk), :]

        hi = logf.astype(BF16)
        r1 = logf - hi.astype(F32)
        mid = r1.astype(BF16)
        lo = (r1 - mid.astype(F32)).astype(BF16)
        b = _dot(tri, hi) + _dot(tri, mid) + _dot(tri, lo)
        b_last = b[chunk - 1:chunk, :]

        st = st_ref[...]
        qe = (qs * jnp.exp(b)).astype(BF16)
        o_inter = lax.dot_general(qe, st.astype(BF16), NT_DIMS, preferred_element_type=F32)

        ref_rows = jnp.concatenate(
            [jnp.broadcast_to(b[blk * sub:blk * sub + 1, :], (sub, HEAD_DIM)) for blk in range(n_blocks)],
            axis=0)
        qt = (qs * jnp.exp(b - ref_rows)).astype(BF16)
        kh = jnp.concatenate(
            [(kk * jnp.exp(jnp.where(row_c < (blk + 1) * sub, b[blk * sub:blk * sub + 1, :] - b, NEG_INF))
              ).astype(BF16) for blk in range(n_blocks)], axis=0)
        res = lax.dot_general(qt, kh, NT_DIMS, preferred_element_type=F32)
        a = jnp.where(keep, res, 0.0).astype(BF16)
        o = o_inter + _dot(a, jnp.concatenate([vb] * n_blocks, axis=0))

        kd = (kk * jnp.exp(b_last - b)).astype(BF16)
        st_ref[...] = st * jnp.exp(b_last) + lax.dot_general(vb, kd, TN_DIMS, preferred_element_type=F32)

        ms = jnp.mean(o * o, axis=-1, keepdims=True)
        on = o * lax.rsqrt(ms + RMS_EPS) * gain
        og = og_ref[pl.ds(off, chunk), :].astype(F32)
        o_ref[pl.ds(off, chunk), :] = (on * jax.nn.sigmoid(og)).astype(o_ref.dtype)
        return carry

    lax.fori_loop(0, tt // chunk, one_chunk, 0, unroll=unroll)


def _sigmoid(x):
    return 0.5 * jnp.tanh(0.5 * x) + 0.5


def _rows_to_blocks(rows, sub):
    return jnp.concatenate([jnp.broadcast_to(rows[k:k + 1, :], (sub, HEAD_DIM)) for k in range(rows.shape[0])],
                           axis=0)


def _hgrn_staged_kernel(q_ref, f_ref, i_ref, og_ref, lb_ref, gain_ref, o_ref, st_ref, *, tt, span, chunk, sub, hp):
    @pl.when(pl.program_id(2) == 0)
    def _():
        st_ref[...] = jnp.zeros_like(st_ref)

    n_chunks = span // chunk
    nblk = chunk // sub
    n_all = span // sub
    stack = nblk * chunk
    r_s = lax.broadcasted_iota(jnp.int32, (span, span), 0)
    c_s = lax.broadcasted_iota(jnp.int32, (span, span), 1)
    tri = ((r_s >= c_s) & (r_s // chunk == c_s // chunk)).astype(BF16)
    r_i = lax.broadcasted_iota(jnp.int32, (chunk, stack), 0)
    c_i = lax.broadcasted_iota(jnp.int32, (chunk, stack), 1)
    keep = ((c_i // chunk) == (r_i // sub)) & ((c_i % chunk) <= r_i)
    pair = lax.broadcasted_iota(jnp.int32, (nblk * nblk, HEAD_DIM), 0)
    pair_ok = (pair % nblk) <= (pair // nblk)
    heads = range(hp)
    cols = [slice(h * HEAD_DIM, (h + 1) * HEAD_DIM) for h in heads]
    f_mid = [0.5 + 0.5 * lb_ref[h] for h in heads]
    f_amp = [0.5 - 0.5 * lb_ref[h] for h in heads]

    def one_span(sidx, carry):
        rows = pl.ds(pl.multiple_of(sidx * span, span), span)

        qs, kk, vb, b = [], [], [], []
        for h in heads:
            xq = q_ref[rows, cols[h]].astype(F32)
            qs.append(xq * _sigmoid(xq))
            th = jnp.tanh(0.5 * f_ref[rows, cols[h]].astype(F32))
            logf = jnp.log(f_mid[h] + f_amp[h] * th)
            kk.append(f_amp[h] - f_amp[h] * th)
            vb.append(i_ref[rows, cols[h]])
            hi = logf.astype(BF16)
            r1 = logf - hi.astype(F32)
            mid = r1.astype(BF16)
            lo = (r1 - mid.astype(F32)).astype(BF16)
            b.append(_dot(tri, hi) + _dot(tri, mid) + _dot(tri, lo))

        qe, b_last, res, upd = [], [], [], []
        for h in heads:
            first = jnp.concatenate([b[h][k * sub:k * sub + 1, :] for k in range(n_all)], axis=0)
            last = jnp.concatenate([b[h][k * sub + sub - 1:(k + 1) * sub, :] for k in range(n_all)], axis=0)
            qt = qs[h] * jnp.exp(b[h] - _rows_to_blocks(first, sub))
            kt = kk[h] * jnp.exp(_rows_to_blocks(last, sub) - b[h])
            bl = [last[(c + 1) * nblk - 1:(c + 1) * nblk, :] for c in range(n_chunks)]
            to_end = jnp.concatenate([bl[c] - last[c * nblk:(c + 1) * nblk, :] for c in range(n_chunks)], axis=0)
            qe.append((qt * _rows_to_blocks(jnp.exp(first), sub)).astype(BF16))
            kd = (kt * _rows_to_blocks(jnp.exp(to_end), sub)).astype(BF16)
            qtb = qt.astype(BF16)
            b_last.append(bl)
            res_h, upd_h = [], []
            for c in range(n_chunks):
                lo_r, hi_r = c * chunk, (c + 1) * chunk
                fr = first[c * nblk:(c + 1) * nblk, :]
                la = last[c * nblk:(c + 1) * nblk, :]
                g = jnp.where(pair_ok,
                              jnp.exp(_rows_to_blocks(fr, nblk) - jnp.concatenate([la] * nblk, axis=0)), 0.0)
                ktc = kt[lo_r:hi_r, :]
                kh = jnp.concatenate(
                    [ktc[jb * sub:(jb + 1) * sub, :] * g[ib * nblk + jb:ib * nblk + jb + 1, :]
                     for ib in range(nblk) for jb in range(nblk)], axis=0).astype(BF16)
                res_h.append(lax.dot_general(qtb[lo_r:hi_r, :], kh, NT_DIMS, preferred_element_type=F32))
                upd_h.append(lax.dot_general(vb[h][lo_r:hi_r, :], kd[lo_r:hi_r, :], TN_DIMS,
                                             preferred_element_type=F32))
            res.append(res_h)
            upd.append(upd_h)

        intra = [[_dot(jnp.where(keep, res[h][c], 0.0).astype(BF16),
                       jnp.concatenate([vb[h][c * chunk:(c + 1) * chunk, :]] * nblk, axis=0))
                  for c in range(n_chunks)] for h in heads]

        st = [st_ref[h] for h in heads]
        for c in range(n_chunks):
            lo_r, hi_r = c * chunk, (c + 1) * chunk
            out_rows = pl.ds(pl.multiple_of(sidx * span + lo_r, chunk), chunk)
            for h in heads:
                o = intra[h][c] + lax.dot_general(qe[h][lo_r:hi_r, :], st[h].astype(BF16), NT_DIMS,
                                                  preferred_element_type=F32)
                st[h] = st[h] * jnp.exp(b_last[h][c]) + upd[h][c]
                ms = jnp.mean(o * o, axis=-1, keepdims=True)
                on = o * lax.rsqrt(ms + RMS_EPS) * gain_ref[h]
                og = og_ref[out_rows, cols[h]].astype(F32)
                o_ref[out_rows, cols[h]] = (on * _sigmoid(og)).astype(o_ref.dtype)
        for h in heads:
            st_ref[h] = st[h]
        return carry

    lax.fori_loop(0, tt // span, one_span, 0)


def _hgrn_call(z, lb, gain, *, bsz, seq, heads, q_col, f_col, i_col, og_col, chunk, sub, unroll):
    staged = sub > 1
    hp = math.gcd(heads, HGRN_HEADS_PER_STEP) if staged else 1
    tt = _tile(seq, 1024 if staged else 256)
    nt = seq // tt
    if staged:
        kern = functools.partial(_hgrn_staged_kernel, tt=tt, span=_tile(tt, HGRN_SPAN), chunk=chunk, sub=sub,
                                 hp=hp)
        head_spec = pl.BlockSpec((hp, 1, HEAD_DIM), lambda b, h, t: (h, 0, 0))
        state = pltpu.VMEM((hp, HEAD_DIM, HEAD_DIM), F32)
    else:
        kern = functools.partial(_hgrn_kernel, tt=tt, chunk=chunk, sub=sub, unroll=unroll)
        head_spec = pl.BlockSpec((None, 1, HEAD_DIM), lambda b, h, t: (h, 0, 0))
        state = pltpu.VMEM((HEAD_DIM, HEAD_DIM), F32)
    width = hp * HEAD_DIM
    assert all(col % hp == 0 for col in (q_col, f_col, i_col, og_col))

    def col_spec(col):
        return pl.BlockSpec((tt, width), lambda b, h, t: (b * nt + t, col // hp + h))

    return pl.pallas_call(
        kern,
        grid=(bsz, heads // hp, nt),
        in_specs=[col_spec(q_col), col_spec(f_col), col_spec(i_col), col_spec(og_col), head_spec, head_spec],
        out_specs=pl.BlockSpec((tt, width), lambda b, h, t: (b * nt + t, h)),
        out_shape=jax.ShapeDtypeStruct((bsz * seq, heads * HEAD_DIM), BF16),
        scratch_shapes=[state],
        compiler_params=_params("parallel", "parallel", "arbitrary"),
        name=f"hgrn_c{chunk}",
    )(z, z, z, z, lb, gain)


def _hgrn(z, lb, gain, **kw):
    fast = functools.partial(_hgrn_call, chunk=HGRN_CHUNK, sub=HGRN_SUB, unroll=True, **kw)
    safe = functools.partial(_hgrn_call, chunk=HGRN_SAFE_CHUNK, sub=1, unroll=False, **kw)
    fast_ok = -(HGRN_SUB - 1) * jnp.log(jnp.min(lb)) <= HGRN_MAX_EXPONENT
    return lax.cond(fast_ok, fast, safe, z, lb, gain)


def _mixout_kernel(*refs, alpha, n_gate, gw, nb):
    oa_lo_ref, oa_hi_ref, ob_ref = refs[:3]
    ga_refs = refs[3:3 + n_gate]
    gb_refs = refs[3 + n_gate:3 + 2 * n_gate]
    x_ref, wpa_ref, wpb_ref, wo_ref, g_ref, b_ref, o_ref = refs[3 + 2 * n_gate:]
    in_lo = pl.program_id(0) % nb < nb // 2
    oa_t = jnp.where(in_lo, oa_lo_ref[...], oa_hi_ref[...])
    ob = ob_ref[...]
    ya = [lax.dot_general(oa_t, wpa_ref[:, c * gw:(c + 1) * gw], TN_DIMS, preferred_element_type=F32)
          for c in range(n_gate)]
    yb = [_dot(ob, wpb_ref[:, c * gw:(c + 1) * gw]) for c in range(n_gate)]
    mix = None
    for c in range(n_gate):
        merged = (_sigmoid(ga_refs[c][...].astype(F32)) * ya[c]
                  + _sigmoid(gb_refs[c][...].astype(F32)) * yb[c])
        part = _dot(merged.astype(BF16), wo_ref[c * gw:(c + 1) * gw, :])
        mix = part if mix is None else mix + part
    o_ref[...] = _layer_norm(alpha * x_ref[...] + mix, g_ref[...], b_ref[...])


def _mixout(o_a_halves, o_b, z, x, wpa, wpb, wo, ln_g, ln_b, *, alpha, ga_off, gb_off, nb):
    m, d = x.shape
    gw = math.gcd(math.gcd(ga_off, gb_off), d)
    n_gate = d // gw
    tm = MOBA_BLOCK
    half = nb // 2
    kern = functools.partial(_mixout_kernel, alpha=alpha, n_gate=n_gate, gw=gw, nb=nb)

    def gate_spec(off, c):
        return pl.BlockSpec((tm, gw), lambda i: (i, off // gw + c))

    def whole(a):
        return pl.BlockSpec(a.shape, lambda i: (0,) * a.ndim, pipeline_mode=pl.Buffered(1))

    def rows(a):
        return pl.BlockSpec((tm, a.shape[1]), lambda i: (i, 0))

    def attn_spec(pos):
        return pl.BlockSpec((o_a_halves[0].shape[0], tm), lambda i: (0, (i // nb) * half + pos(i % nb)))

    return pl.pallas_call(
        kern,
        grid=(m // tm,),
        in_specs=([attn_spec(lambda blk: jnp.minimum(blk, half - 1)),
                   attn_spec(lambda blk: nb - 1 - jnp.maximum(blk, half)), rows(o_b)]
                  + [gate_spec(ga_off, c) for c in range(n_gate)]
                  + [gate_spec(gb_off, c) for c in range(n_gate)]
                  + [rows(x), whole(wpa), whole(wpb), whole(wo), whole(ln_g), whole(ln_b)]),
        out_specs=pl.BlockSpec((tm, d), lambda i: (i, 0)),
        out_shape=jax.ShapeDtypeStruct((m, d), F32),
        compiler_params=_params("parallel"),
        name="mixout",
    )(*o_a_halves, o_b, *([z] * (2 * n_gate)), x, wpa, wpb, wo, ln_g, ln_b)


def _ffn_kernel(h_ref, wg_ref, wu_ref, wd_ref, g_ref, b_ref, o_ref, hb_ref, *, alpha, nf):
    f = pl.program_id(1)

    @pl.when(f == 0)
    def _():
        hb_ref[...] = h_ref[...].astype(BF16)
        o_ref[...] = jnp.zeros_like(o_ref)

    chunk = min(FFN_ROW_CHUNK, hb_ref.shape[0])
    for r in range(0, hb_ref.shape[0], chunk):
        rows = slice(r, r + chunk)
        hb = hb_ref[rows, :]
        gate = _dot(hb, wg_ref[...])
        up = _dot(hb, wu_ref[...])
        act = (gate * _sigmoid(gate) * up).astype(BF16)
        o_ref[rows, :] += _dot(act, wd_ref[...])

    @pl.when(f == nf - 1)
    def _():
        for r in range(0, hb_ref.shape[0], chunk):
            rows = slice(r, r + chunk)
            o_ref[rows, :] = _layer_norm(alpha * h_ref[rows, :] + o_ref[rows, :], g_ref[...], b_ref[...])


def _ffn(h, wg, wu, wd, ln_g, ln_b, *, alpha):
    m, d = h.shape
    hidden = wg.shape[1]
    tm, tf = _tile(m, 1024), _tile(hidden, 512)
    assert tm % min(FFN_ROW_CHUNK, tm) == 0
    nf = hidden // tf
    kern = functools.partial(_ffn_kernel, alpha=alpha, nf=nf)
    vec = pl.BlockSpec((1, d), lambda i, f: (0, 0))
    return pl.pallas_call(
        kern,
        grid=(m // tm, nf),
        in_specs=[pl.BlockSpec((tm, d), lambda i, f: (i, 0)),
                  pl.BlockSpec((d, tf), lambda i, f: (0, f)),
                  pl.BlockSpec((d, tf), lambda i, f: (0, f)),
                  pl.BlockSpec((tf, d), lambda i, f: (f, 0)),
                  vec, vec],
        out_specs=pl.BlockSpec((tm, d), lambda i, f: (i, 0)),
        out_shape=jax.ShapeDtypeStruct((m, d), F32),
        scratch_shapes=[pltpu.VMEM((tm, d), BF16)],
        compiler_params=_params("parallel", "arbitrary"),
        name="ffn",
    )(h, wg, wu, wd, ln_g, ln_b)


def kernel(x, w_in, w_proj_a, w_proj_b, w_out, hgrn_norm_g, hgrn_lb_logits, ln1_g, ln1_b,
           w_gate_ffn, w_up_ffn, w_down_ffn, ln2_g, ln2_b):
    bsz, seq, d = x.shape
    depth = w_in.shape[0]
    a_width = w_proj_a.shape[1]
    b_vwidth = w_proj_b.shape[1]
    b_fwidth = hgrn_lb_logits.shape[1]
    assert a_width % HEAD_DIM == 0 and b_vwidth == b_fwidth and b_fwidth % HEAD_DIM == 0
    assert w_in.shape[2] == 3 * a_width + 2 * b_fwidth + 2 * b_vwidth + 2 * d
    a_heads = a_width // HEAD_DIM
    b_heads = b_fwidth // HEAD_DIM
    qa_col, ka_col, va_col = 0, a_heads, 2 * a_heads
    qb_col = 3 * a_heads
    fb_col, ib_col, ogb_col = qb_col + b_heads, qb_col + 2 * b_heads, qb_col + 3 * b_heads
    ga_off = 3 * a_width + 2 * b_fwidth + 2 * b_vwidth
    gb_off = ga_off + d
    alpha = (2.0 * depth) ** 0.25

    lb_all = jnp.cumsum(jax.nn.softmax(hgrn_lb_logits.astype(F32), axis=0), axis=0)

    h = x.reshape(bsz * seq, d)
    for l in range(depth):
        z = _inproj(h, w_in[l].astype(BF16))
        o_a = _moba(z, bsz, seq, a_heads, qa_col, ka_col, va_col)
        o_b = _hgrn(z, lb_all[l].reshape(b_heads, 1, HEAD_DIM),
                    hgrn_norm_g[l].astype(F32).reshape(b_heads, 1, HEAD_DIM),
                    bsz=bsz, seq=seq, heads=b_heads,
                    q_col=qb_col, f_col=fb_col, i_col=ib_col, og_col=ogb_col)
        h = _mixout(o_a, o_b, z, h, w_proj_a[l].astype(BF16), w_proj_b[l].astype(BF16),
                    w_out[l].astype(BF16), ln1_g[l].reshape(1, d), ln1_b[l].reshape(1, d),
                    alpha=alpha, ga_off=ga_off, gb_off=gb_off, nb=seq // MOBA_BLOCK)
        h = _ffn(h, w_gate_ffn[l].astype(BF16), w_up_ffn[l].astype(BF16), w_down_ffn[l].astype(BF16),
                 ln2_g[l].reshape(1, d), ln2_b[l].reshape(1, d), alpha=alpha)
    return h.reshape(bsz, seq, d)
```

```python
import functools
import math

import jax
import jax.numpy as jnp
from jax import lax
from jax.experimental import pallas as pl
from jax.experimental.pallas import tpu as pltpu

F32 = jnp.float32
BF16 = jnp.bfloat16

HEAD_DIM = 128
MOBA_BLOCK = 256
MOBA_TOPK = 3
MOBA_GROUP = 2
MOBA_HEADS_PER_STEP = 2
MOBA_MAX_FLOOR = float(jnp.finfo(jnp.float32).min)
MOBA_ONES_ROWS = 16
LOG2_E = 1.4426950408889634
LN_EPS = 1e-5
RMS_EPS = 1e-6
NEG_INF = float("-inf")

HGRN_CHUNK = 64
HGRN_SUB = 16
HGRN_SPAN = 256
HGRN_HEADS_PER_STEP = 4
HGRN_SAFE_CHUNK = 8
HGRN_MAX_EXPONENT = 60.0

VMEM_LIMIT_BYTES = 56 * 1024 * 1024
FFN_ROW_CHUNK = 512

NT_DIMS = (((1,), (1,)), ((), ()))
TN_DIMS = (((0,), (0,)), ((), ()))


def _tile(n, pref):
    if n <= pref:
        return n
    t = pref - pref % 128
    while t >= 128:
        if n % t == 0:
            return t
        t -= 128
    raise ValueError(f"no 128-aligned tile for {n}")


def _params(*sem):
    return pltpu.CompilerParams(dimension_semantics=sem, vmem_limit_bytes=VMEM_LIMIT_BYTES)


def _aligned(offset, multiple):
    return offset if isinstance(offset, int) else pl.multiple_of(offset, multiple)


def _dot(a, b):
    return jnp.dot(a, b, preferred_element_type=F32)


def _layer_norm(pre, g, b):
    mu = jnp.mean(pre, axis=-1, keepdims=True)
    d = pre - mu
    var = jnp.mean(d * d, axis=-1, keepdims=True)
    return d * lax.rsqrt(var + LN_EPS) * g + b


def _inproj_kernel(x_ref, w_ref, o_ref, xb_ref):
    @pl.when(pl.program_id(1) == 0)
    def _():
        xb_ref[...] = x_ref[...].astype(BF16)

    o_ref[...] = _dot(xb_ref[...], w_ref[...]).astype(o_ref.dtype)


def _inproj(x, w):
    m, k = x.shape
    n = w.shape[1]
    tm, tn = _tile(m, 1024), _tile(n, 1024)
    return pl.pallas_call(
        _inproj_kernel,
        grid=(m // tm, n // tn),
        in_specs=[pl.BlockSpec((tm, k), lambda i, j: (i, 0)),
                  pl.BlockSpec((k, tn), lambda i, j: (0, j))],
        out_specs=pl.BlockSpec((tm, tn), lambda i, j: (i, j)),
        out_shape=jax.ShapeDtypeStruct((m, n), BF16),
        scratch_shapes=[pltpu.VMEM((tm, k), BF16)],
        compiler_params=_params("parallel", "arbitrary"),
        name="inproj",
    )(x, w)


def _moba_kernel(z_hbm, oa_ref, ob_ref, q_ref, k_ref, v_ref, dma_sem, kmean_ref, vt_ref, qts_ref, sel_ref,
                 sa_ref, sb_ref, ma_ref, mb_ref, pa_ref, pb_ref, acc_ref,
                 *, nb, nbp, log2_scale, hp, q_col, k_col, v_col):
    i = pl.program_id(2)
    blk = MOBA_BLOCK
    grp = MOBA_GROUP
    heads = range(hp)
    cols = [slice(h * HEAD_DIM, (h + 1) * HEAD_DIM) for h in heads]

    @pl.when(i == 0)
    def _():
        width = hp * HEAD_DIM
        copies = [
            pltpu.make_async_copy(
                z_hbm.at[pl.program_id(0), :, pl.ds((col // hp + pl.program_id(1)) * width, width)], dst,
                dma_sem.at[idx])
            for idx, (col, dst) in enumerate(((q_col, q_ref), (k_col, k_ref), (v_col, v_ref)))]
        for cp in copies:
            cp.start()
        for cp in copies:
            cp.wait()
        kmean_ref[...] = jnp.zeros_like(kmean_ref)

        def prep(n, carry):
            rows = pl.ds(pl.multiple_of(n * blk, blk), blk)
            for h in heads:
                kf = k_ref[rows, cols[h]].astype(F32)
                kmean_ref[h, pl.ds(n, 1), :] = jnp.sum(kf, axis=0, keepdims=True) * (1.0 / blk)
                vt_ref[h, n, :HEAD_DIM, :] = v_ref[rows, cols[h]].astype(F32).T.astype(BF16)
                vt_ref[h, n, HEAD_DIM:, :] = jnp.ones((MOBA_ONES_ROWS, blk), BF16)
            return carry

        lax.fori_loop(0, nb, prep, 0)
        row = lax.broadcasted_iota(jnp.int32, (nbp, blk), 0)

        def select(n, carry):
            rows = pl.ds(pl.multiple_of(n * blk, blk), blk)
            for h in heads:
                qt32 = q_ref[rows, cols[h]].astype(F32).T
                qt = qt32.astype(BF16)
                qts_ref[h, n] = (qt32 * log2_scale).astype(BF16)
                km = kmean_ref[h]
                km_hi = km.astype(BF16)
                km_lo = (km - km_hi.astype(F32)).astype(BF16)
                gate = _dot(km_hi, qt) + _dot(km_lo, qt)
                g = jnp.where(row < n, gate, NEG_INF)
                sel = jnp.zeros((nbp, blk), dtype=jnp.bool_)
                for _ in range(MOBA_TOPK):
                    mx = jnp.max(g, axis=0, keepdims=True)
                    idx = jnp.min(jnp.where(g == mx, row, nbp), axis=0, keepdims=True)
                    pick = (row == idx) & (mx > NEG_INF)
                    sel = sel | pick
                    g = jnp.where(pick, NEG_INF, g)
                sel_ref[h, n] = sel.astype(F32)
            return carry

        lax.fori_loop(0, nb, select, 0, unroll=2)

    blk_a, blk_b = i, nb - 1 - i
    n_a = (blk_a + grp) // grp
    n_total = nb // grp + 1

    def group(g):
        in_b = g >= n_a
        qb = jnp.where(in_b, blk_b, blk_a)
        n = jnp.where(in_b, n_total - 1 - g, g)
        ids = [qb - n * grp - c for c in range(grp)]
        return dict(qb=qb, slot=in_b.astype(jnp.int32), own=g in (0, n_total - 1),
                    restart=True if g == 0 else g == n_a,
                    js=[jnp.maximum(j, 0) for j in ids], oks=[(j >= 0).astype(F32) for j in ids])

    def scores_into(sbuf, mxbuf, d):
        for h in heads:
            for c, j in enumerate(d["js"]):
                s = _dot(k_ref[pl.ds(pl.multiple_of(j * blk, blk), blk), cols[h]], qts_ref[h, d["qb"]])
                if d["own"] and c == 0:
                    s = jnp.where(lax.broadcasted_iota(jnp.int32, (blk, blk), 0)
                                  <= lax.broadcasted_iota(jnp.int32, (blk, blk), 1), s, NEG_INF)
                sbuf[h, c] = s
                mxbuf[h, c] = jnp.broadcast_to(jnp.max(s, axis=0, keepdims=True), (8, blk))

    def softmax_into(pbuf, sbuf, mxbuf, d, m_old):
        floor = jnp.full((1, blk), MOBA_MAX_FLOOR, F32)
        m_out, alpha_out = [], []
        for h in heads:
            chosen = [None if (d["own"] and c == 0)
                      else sel_ref[h, d["qb"], pl.ds(d["js"][c], 1), :] * d["oks"][c] > 0.0 for c in range(grp)]
            if d["restart"] is True:
                m_prev = floor
            else:
                m_prev = jnp.where(jnp.full((1, blk), d["restart"].astype(F32)) > 0.0, floor, m_old[h])
            m_new = m_prev
            for c in range(grp):
                cm = mxbuf[h, c, 0:1, :]
                m_new = jnp.maximum(m_new, cm if chosen[c] is None else jnp.where(chosen[c], cm, NEG_INF))
            for c in range(grp):
                mb = m_new if chosen[c] is None else jnp.where(chosen[c], m_new, jnp.inf)
                pbuf[h, c] = jnp.exp2(sbuf[h, c] - mb).astype(BF16)
            m_out.append(m_new)
            alpha_out.append(jnp.exp2(m_prev - m_new))
        return m_out, alpha_out

    def accumulate(pbuf, d, alpha):
        for h in heads:
            pv = None
            for c, j in enumerate(d["js"]):
                part = _dot(vt_ref[h, j], pbuf[h, c])
                pv = part if pv is None else pv + part
            acc_ref[d["slot"], h] = alpha[h] * acc_ref[d["slot"], h] + pv

    s_bufs, mx_bufs, p_bufs = (sa_ref, sb_ref), (ma_ref, mb_ref), (pa_ref, pb_ref)
    acc_ref[...] = jnp.zeros_like(acc_ref)
    groups = [group(g) for g in range(n_total)]
    m = [None for _ in heads]
    alpha = None
    scores_into(s_bufs[0], mx_bufs[0], groups[0])
    for g in range(n_total):
        if g + 1 < n_total:
            scores_into(s_bufs[(g + 1) % 2], mx_bufs[(g + 1) % 2], groups[g + 1])
        if g > 0:
            accumulate(p_bufs[(g - 1) % 2], groups[g - 1], alpha)
        m, alpha = softmax_into(p_bufs[g % 2], s_bufs[g % 2], mx_bufs[g % 2], groups[g], m)
    accumulate(p_bufs[(n_total - 1) % 2], groups[n_total - 1], alpha)
    for slot, out_ref in enumerate((oa_ref, ob_ref)):
        for h in heads:
            num = acc_ref[slot, h, :HEAD_DIM, :]
            den = acc_ref[slot, h, HEAD_DIM:HEAD_DIM + 1, :]
            out_ref[cols[h], :] = (num / den).astype(out_ref.dtype)


def _moba(z, bsz, seq, heads, q_col, k_col, v_col):
    blk = MOBA_BLOCK
    assert seq % (2 * blk) == 0 and MOBA_GROUP == 2
    nb = seq // blk
    half = nb // 2
    nbp = -(-nb // 8) * 8
    acc_rows = HEAD_DIM + MOBA_ONES_ROWS
    hp = math.gcd(heads, MOBA_HEADS_PER_STEP)
    assert all(col % hp == 0 for col in (q_col, k_col, v_col))
    width = hp * HEAD_DIM
    z3 = z.reshape(bsz, seq, z.shape[1])
    kern = functools.partial(_moba_kernel, nb=nb, nbp=nbp, log2_scale=HEAD_DIM ** -0.5 * LOG2_E, hp=hp,
                             q_col=q_col, k_col=k_col, v_col=v_col)
    seq_buf = pltpu.VMEM((seq, width), BF16)
    return pl.pallas_call(
        kern,
        grid=(bsz, heads // hp, half),
        in_specs=[pl.BlockSpec(memory_space=pl.ANY)],
        out_specs=[pl.BlockSpec((width, blk), lambda b, h, i: (h, b * half + i))] * 2,
        out_shape=[jax.ShapeDtypeStruct((heads * HEAD_DIM, bsz * seq // 2), BF16)] * 2,
        scratch_shapes=[seq_buf, seq_buf, seq_buf,
                        pltpu.SemaphoreType.DMA((3,)),
                        pltpu.VMEM((hp, nbp, HEAD_DIM), F32),
                        pltpu.VMEM((hp, nb, acc_rows, blk), BF16),
                        pltpu.VMEM((hp, nb, HEAD_DIM, blk), BF16),
                        pltpu.VMEM((hp, nb, nbp, blk), F32),
                        pltpu.VMEM((hp, MOBA_GROUP, blk, blk), F32),
                        pltpu.VMEM((hp, MOBA_GROUP, blk, blk), F32),
                        pltpu.VMEM((hp, MOBA_GROUP, 8, blk), F32),
                        pltpu.VMEM((hp, MOBA_GROUP, 8, blk), F32),
                        pltpu.VMEM((hp, MOBA_GROUP, blk, blk), BF16),
                        pltpu.VMEM((hp, MOBA_GROUP, blk, blk), BF16),
                        pltpu.VMEM((2, hp, acc_rows, blk), F32)],
        compiler_params=_params("parallel", "parallel", "arbitrary"),
        name="moba",
    )(z3)


def _hgrn_kernel(q_ref, f_ref, i_ref, og_ref, lb_ref, gain_ref, o_ref, st_ref, *, tt, chunk, sub, unroll):
    @pl.when(pl.program_id(2) == 0)
    def _():
        st_ref[...] = jnp.zeros_like(st_ref)

    n_blocks = chunk // sub
    stack = n_blocks * chunk
    lb = lb_ref[...]
    gain = gain_ref[...]
    tri = (lax.broadcasted_iota(jnp.int32, (chunk, chunk), 0)
           >= lax.broadcasted_iota(jnp.int32, (chunk, chunk), 1)).astype(BF16)
    r_i = lax.broadcasted_iota(jnp.int32, (chunk, stack), 0)
    c_i = lax.broadcasted_iota(jnp.int32, (chunk, stack), 1)
    keep = ((c_i // chunk) == (r_i // sub)) & ((c_i % chunk) <= r_i)
    row_c = lax.broadcasted_iota(jnp.int32, (chunk, HEAD_DIM), 0)

    def one_chunk(n, carry):
        off = pl.multiple_of(n * chunk, chunk)
        xq = q_ref[pl.ds(off, chunk), :].astype(F32)
        qs = xq * jax.nn.sigmoid(xq)
        f = lb + (1.0 - lb) * jax.nn.sigmoid(f_ref[pl.ds(off, chunk), :].astype(F32))
        logf = jnp.log(f)
        kk = 1.0 - f
        vb = i_ref[pl.ds(off, chunk), :]

        hi = logf.astype(BF16)
        r1 = logf - hi.astype(F32)
        mid = r1.astype(BF16)
        lo = (r1 - mid.astype(F32)).astype(BF16)
        b = _dot(tri, hi) + _dot(tri, mid) + _dot(tri, lo)
        b_last = b[chunk - 1:chunk, :]

        st = st_ref[...]
        qe = (qs * jnp.exp(b)).astype(BF16)
        o_inter = lax.dot_general(qe, st.astype(BF16), NT_DIMS, preferred_element_type=F32)

        ref_rows = jnp.concatenate(
            [jnp.broadcast_to(b[blk * sub:blk * sub + 1, :], (sub, HEAD_DIM)) for blk in range(n_blocks)],
            axis=0)
        qt = (qs * jnp.exp(b - ref_rows)).astype(BF16)
        kh = jnp.concatenate(
            [(kk * jnp.exp(jnp.where(row_c < (blk + 1) * sub, b[blk * sub:blk * sub + 1, :] - b, NEG_INF))
              ).astype(BF16) for blk in range(n_blocks)], axis=0)
        res = lax.dot_general(qt, kh, NT_DIMS, preferred_element_type=F32)
        a = jnp.where(keep, res, 0.0).astype(BF16)
        o = o_inter + _dot(a, jnp.concatenate([vb] * n_blocks, axis=0))

        kd = (kk * jnp.exp(b_last - b)).astype(BF16)
        st_ref[...] = st * jnp.exp(b_last) + lax.dot_general(vb, kd, TN_DIMS, preferred_element_type=F32)

        ms = jnp.mean(o * o, axis=-1, keepdims=True)
        on = o * lax.rsqrt(ms + RMS_EPS) * gain
        og = og_ref[pl.ds(off, chunk), :].astype(F32)
        o_ref[pl.ds(off, chunk), :] = (on * jax.nn.sigmoid(og)).astype(o_ref.dtype)
        return carry

    lax.fori_loop(0, tt // chunk, one_chunk, 0, unroll=unroll)


def _sigmoid(x):
    return 0.5 * jnp.tanh(0.5 * x) + 0.5


def _rows_to_blocks(rows, sub):
    return jnp.concatenate([jnp.broadcast_to(rows[k:k + 1, :], (sub, HEAD_DIM)) for k in range(rows.shape[0])],
                           axis=0)


def _hgrn_staged_kernel(q_ref, f_ref, i_ref, og_ref, lb_ref, gain_ref, o_ref, st_ref, *, tt, span, chunk, sub, hp):
    @pl.when(pl.program_id(2) == 0)
    def _():
        st_ref[...] = jnp.zeros_like(st_ref)

    _hgrn_spans(q_ref, f_ref, i_ref, og_ref, lb_ref, gain_ref, o_ref, st_ref, n_spans=tt // span, span=span,
                chunk=chunk, sub=sub, hp=hp)


def _hgrn_spans(q_ref, f_ref, i_ref, og_ref, lb_ref, gain_ref, o_ref, st_ref, *, n_spans, span, chunk, sub, hp,
                hooks=()):
    n_chunks = span // chunk
    nblk = chunk // sub
    n_all = span // sub
    stack = nblk * chunk
    r_s = lax.broadcasted_iota(jnp.int32, (span, span), 0)
    c_s = lax.broadcasted_iota(jnp.int32, (span, span), 1)
    tri = ((r_s >= c_s) & (r_s // chunk == c_s // chunk)).astype(BF16)
    r_i = lax.broadcasted_iota(jnp.int32, (chunk, stack), 0)
    c_i = lax.broadcasted_iota(jnp.int32, (chunk, stack), 1)
    keep = ((c_i // chunk) == (r_i // sub)) & ((c_i % chunk) <= r_i)
    pair = lax.broadcasted_iota(jnp.int32, (nblk * nblk, HEAD_DIM), 0)
    pair_ok = (pair % nblk) <= (pair // nblk)
    heads = range(hp)
    cols = [slice(h * HEAD_DIM, (h + 1) * HEAD_DIM) for h in heads]
    f_mid = [0.5 + 0.5 * lb_ref[h] for h in heads]
    f_amp = [0.5 - 0.5 * lb_ref[h] for h in heads]

    def hook(k):
        if k < len(hooks):
            hooks[k]()

    def one_span(sidx, carry):
        rows = pl.ds(_aligned(sidx * span, span), span)

        qs, kk, vb, b = [], [], [], []
        for h in heads:
            xq = q_ref[rows, cols[h]].astype(F32)
            qs.append(xq * _sigmoid(xq))
            th = jnp.tanh(0.5 * f_ref[rows, cols[h]].astype(F32))
            logf = jnp.log(f_mid[h] + f_amp[h] * th)
            kk.append(f_amp[h] - f_amp[h] * th)
            vb.append(i_ref[rows, cols[h]])
            hi = logf.astype(BF16)
            r1 = logf - hi.astype(F32)
            mid = r1.astype(BF16)
            lo = (r1 - mid.astype(F32)).astype(BF16)
            b.append(_dot(tri, hi) + _dot(tri, mid) + _dot(tri, lo))

        hook(0)
        qe, b_last, res, upd = [], [], [], []
        for h in heads:
            first = jnp.concatenate([b[h][k * sub:k * sub + 1, :] for k in range(n_all)], axis=0)
            last = jnp.concatenate([b[h][k * sub + sub - 1:(k + 1) * sub, :] for k in range(n_all)], axis=0)
            qt = qs[h] * jnp.exp(b[h] - _rows_to_blocks(first, sub))
            kt = kk[h] * jnp.exp(_rows_to_blocks(last, sub) - b[h])
            bl = [last[(c + 1) * nblk - 1:(c + 1) * nblk, :] for c in range(n_chunks)]
            to_end = jnp.concatenate([bl[c] - last[c * nblk:(c + 1) * nblk, :] for c in range(n_chunks)], axis=0)
            qe.append((qt * _rows_to_blocks(jnp.exp(first), sub)).astype(BF16))
            kd = (kt * _rows_to_blocks(jnp.exp(to_end), sub)).astype(BF16)
            qtb = qt.astype(BF16)
            b_last.append(bl)
            res_h, upd_h = [], []
            for c in range(n_chunks):
                lo_r, hi_r = c * chunk, (c + 1) * chunk
                fr = first[c * nblk:(c + 1) * nblk, :]
                la = last[c * nblk:(c + 1) * nblk, :]
                g = jnp.where(pair_ok,
                              jnp.exp(_rows_to_blocks(fr, nblk) - jnp.concatenate([la] * nblk, axis=0)), 0.0)
                ktc = kt[lo_r:hi_r, :]
                kh = jnp.concatenate(
                    [ktc[jb * sub:(jb + 1) * sub, :] * g[ib * nblk + jb:ib * nblk + jb + 1, :]
                     for ib in range(nblk) for jb in range(nblk)], axis=0).astype(BF16)
                res_h.append(lax.dot_general(qtb[lo_r:hi_r, :], kh, NT_DIMS, preferred_element_type=F32))
                upd_h.append(lax.dot_general(vb[h][lo_r:hi_r, :], kd[lo_r:hi_r, :], TN_DIMS,
                                             preferred_element_type=F32))
            res.append(res_h)
            upd.append(upd_h)

        hook(1)
        intra = [[_dot(jnp.where(keep, res[h][c], 0.0).astype(BF16),
                       jnp.concatenate([vb[h][c * chunk:(c + 1) * chunk, :]] * nblk, axis=0))
                  for c in range(n_chunks)] for h in heads]

        hook(2)
        st = [st_ref[h] for h in heads]
        for c in range(n_chunks):
            lo_r, hi_r = c * chunk, (c + 1) * chunk
            out_rows = pl.ds(_aligned(sidx * span + lo_r, chunk), chunk)
            for h in heads:
                o = intra[h][c] + lax.dot_general(qe[h][lo_r:hi_r, :], st[h].astype(BF16), NT_DIMS,
                                                  preferred_element_type=F32)
                st[h] = st[h] * jnp.exp(b_last[h][c]) + upd[h][c]
                ms = jnp.mean(o * o, axis=-1, keepdims=True)
                on = o * lax.rsqrt(ms + RMS_EPS) * gain_ref[h]
                og = og_ref[out_rows, cols[h]].astype(F32)
                o_ref[out_rows, cols[h]] = (on * _sigmoid(og)).astype(o_ref.dtype)
            hook(3 + c)
        for h in heads:
            st_ref[h] = st[h]
        return carry

    if n_spans == 1:
        one_span(0, 0)
    else:
        lax.fori_loop(0, n_spans, one_span, 0)


def _hgrn_call(z, lb, gain, *, bsz, seq, heads, q_col, f_col, i_col, og_col, chunk, sub, unroll):
    staged = sub > 1
    hp = math.gcd(heads, HGRN_HEADS_PER_STEP) if staged else 1
    tt = _tile(seq, 1024 if staged else 256)
    nt = seq // tt
    if staged:
        kern = functools.partial(_hgrn_staged_kernel, tt=tt, span=_tile(tt, HGRN_SPAN), chunk=chunk, sub=sub,
                                 hp=hp)
        head_spec = pl.BlockSpec((hp, 1, HEAD_DIM), lambda b, h, t: (h, 0, 0))
        state = pltpu.VMEM((hp, HEAD_DIM, HEAD_DIM), F32)
    else:
        kern = functools.partial(_hgrn_kernel, tt=tt, chunk=chunk, sub=sub, unroll=unroll)
        head_spec = pl.BlockSpec((None, 1, HEAD_DIM), lambda b, h, t: (h, 0, 0))
        state = pltpu.VMEM((HEAD_DIM, HEAD_DIM), F32)
    width = hp * HEAD_DIM
    assert all(col % hp == 0 for col in (q_col, f_col, i_col, og_col))

    def col_spec(col):
        return pl.BlockSpec((tt, width), lambda b, h, t: (b * nt + t, col // hp + h))

    return pl.pallas_call(
        kern,
        grid=(bsz, heads // hp, nt),
        in_specs=[col_spec(q_col), col_spec(f_col), col_spec(i_col), col_spec(og_col), head_spec, head_spec],
        out_specs=pl.BlockSpec((tt, width), lambda b, h, t: (b * nt + t, h)),
        out_shape=jax.ShapeDtypeStruct((bsz * seq, heads * HEAD_DIM), BF16),
        scratch_shapes=[state],
        compiler_params=_params("parallel", "parallel", "arbitrary"),
        name=f"hgrn_c{chunk}",
    )(z, z, z, z, lb, gain)


def _hgrn_fast_ok(lb):
    return -(HGRN_SUB - 1) * jnp.log(jnp.min(lb)) <= HGRN_MAX_EXPONENT


def _hgrn(z, lb, gain, **kw):
    fast = functools.partial(_hgrn_call, chunk=HGRN_CHUNK, sub=HGRN_SUB, unroll=True, **kw)
    safe = functools.partial(_hgrn_call, chunk=HGRN_SAFE_CHUNK, sub=1, unroll=False, **kw)
    return lax.cond(_hgrn_fast_ok(lb), fast, safe, z, lb, gain)


def _mixout_kernel(*refs, alpha, n_gate, gw, nb, hgrn):
    if hgrn:
        (q_ref, f_ref, i_ref, og_ref, lb_ref, gain_ref), refs = refs[:6], refs[6:]
    (oa_lo_ref, oa_hi_ref), refs = refs[:2], refs[2:]
    if not hgrn:
        ob_in_ref, refs = refs[0], refs[1:]
    ga_refs, gb_refs, refs = refs[:n_gate], refs[n_gate:2 * n_gate], refs[2 * n_gate:]
    x_ref, wpa_ref, wpb_ref, wo_ref, g_ref, b_ref, o_ref = refs[:7]
    step = pl.program_id(0)
    if hgrn:
        st_ref, ob_buf_ref = refs[7:]
        tile = jnp.maximum(step - 1, 0)
        next_tile = jnp.minimum(step, pl.num_programs(0) - 2)
        slot = step % 2

        @pl.when(step == 0)
        def _():
            ob_buf_ref[...] = jnp.zeros_like(ob_buf_ref)

        @pl.when(next_tile % nb == 0)
        def _():
            st_ref[...] = jnp.zeros_like(st_ref)

        ob = ob_buf_ref[1 - slot]
    else:
        tile = step
        ob = ob_in_ref[...]
    in_lo = tile % nb < nb // 2
    oa_t = jnp.where(in_lo, oa_lo_ref[...], oa_hi_ref[...])
    merged, parts = {}, []

    def project(c):
        ya = lax.dot_general(oa_t, wpa_ref[:, c * gw:(c + 1) * gw], TN_DIMS, preferred_element_type=F32)
        yb = _dot(ob, wpb_ref[:, c * gw:(c + 1) * gw])
        merged[c] = (jax.nn.sigmoid(ga_refs[c][...].astype(F32)) * ya
                     + jax.nn.sigmoid(gb_refs[c][...].astype(F32)) * yb).astype(BF16)

    def output(c):
        parts.append(_dot(merged[c], wo_ref[c * gw:(c + 1) * gw, :]))

    def finish():
        o_ref[...] = _layer_norm(alpha * x_ref[...] + sum(parts[1:], parts[0]), g_ref[...], b_ref[...])

    def chain(*fs):
        return lambda: [f() for f in fs]

    steps = [functools.partial(project, 0)]
    steps += [chain(functools.partial(output, c - 1), functools.partial(project, c)) for c in range(1, n_gate)]
    pieces = steps + [functools.partial(output, n_gate - 1), finish]
    if hgrn:
        n_hooks = 3 + ob_buf_ref.shape[1] // hgrn["chunk"]
        _hgrn_spans(q_ref, f_ref, i_ref, og_ref, lb_ref, gain_ref, ob_buf_ref.at[slot], st_ref, n_spans=1,
                    span=ob_buf_ref.shape[1], hooks=pieces[:n_hooks], **hgrn)
        pieces = pieces[n_hooks:]
    for piece in pieces:
        piece()


def _mixout(o_a_halves, o_b, z, x, wpa, wpb, wo, ln_g, ln_b, *, alpha, ga_off, gb_off, nb, hgrn=None):
    m, d = x.shape
    gw = math.gcd(math.gcd(ga_off, gb_off), d)
    n_gate = d // gw
    tm = MOBA_BLOCK
    half = nb // 2
    n_tiles = m // tm
    if hgrn:
        hb = hgrn["heads"]
        b_width = hb * HEAD_DIM
        assert tm % HGRN_CHUNK == 0 and all(hgrn[k] % hb == 0 for k in ("q_col", "f_col", "i_col", "og_col"))
        mix_tile = lambda i: jnp.maximum(i - 1, 0)
        next_tile = lambda i: jnp.minimum(i, n_tiles - 1)
        settings = dict(chunk=HGRN_CHUNK, sub=HGRN_SUB, hp=hb)
    else:
        mix_tile = lambda i: i
        settings = None
    kern = functools.partial(_mixout_kernel, alpha=alpha, n_gate=n_gate, gw=gw, nb=nb, hgrn=settings)

    def gate_spec(off, c):
        return pl.BlockSpec((tm, gw), lambda i: (mix_tile(i), off // gw + c))

    def whole(a):
        return pl.BlockSpec(a.shape, lambda i: (0,) * a.ndim, pipeline_mode=pl.Buffered(1))

    def rows(a):
        return pl.BlockSpec((tm, a.shape[1]), lambda i: (mix_tile(i), 0))

    def attn_spec(pos):
        def index(i):
            t = mix_tile(i)
            return 0, (t // nb) * half + pos(t % nb)
        return pl.BlockSpec((o_a_halves[0].shape[0], tm), index)

    attn_specs = [attn_spec(lambda blk: jnp.minimum(blk, half - 1)),
                  attn_spec(lambda blk: nb - 1 - jnp.maximum(blk, half))]
    tail_specs = ([gate_spec(ga_off, c) for c in range(n_gate)] + [gate_spec(gb_off, c) for c in range(n_gate)]
                  + [rows(x), whole(wpa), whole(wpb), whole(wo), whole(ln_g), whole(ln_b)])
    tail_args = (*([z] * (2 * n_gate)), x, wpa, wpb, wo, ln_g, ln_b)
    if hgrn:
        def col_spec(col):
            return pl.BlockSpec((tm, b_width), lambda i: (next_tile(i), col // hb))
        in_specs = ([col_spec(hgrn[k]) for k in ("q_col", "f_col", "i_col", "og_col")]
                    + [whole(hgrn["lb"]), whole(hgrn["gain"])] + attn_specs + tail_specs)
        args = (z, z, z, z, hgrn["lb"], hgrn["gain"], *o_a_halves, *tail_args)
        scratch = [pltpu.VMEM((hb, HEAD_DIM, HEAD_DIM), F32),
                   pltpu.VMEM((2, tm, b_width), BF16)]
    else:
        in_specs = attn_specs + [rows(o_b)] + tail_specs
        args = (*o_a_halves, o_b, *tail_args)
        scratch = []
    return pl.pallas_call(
        kern,
        grid=(n_tiles + (1 if hgrn else 0),),
        in_specs=in_specs,
        out_specs=pl.BlockSpec((tm, d), lambda i: (mix_tile(i), 0)),
        out_shape=jax.ShapeDtypeStruct((m, d), F32),
        scratch_shapes=scratch,
        compiler_params=_params("arbitrary"),
        name="hgrn_mixout" if hgrn else "mixout",
    )(*args)


def _ffn_kernel(h_ref, wg_ref, wu_ref, wd_ref, g_ref, b_ref, o_ref, hb_ref, *, alpha, nf):
    f = pl.program_id(1)

    @pl.when(f == 0)
    def _():
        hb_ref[...] = h_ref[...].astype(BF16)
        o_ref[...] = jnp.zeros_like(o_ref)

    chunk = min(FFN_ROW_CHUNK, hb_ref.shape[0])
    for r in range(0, hb_ref.shape[0], chunk):
        rows = slice(r, r + chunk)
        hb = hb_ref[rows, :]
        gate = _dot(hb, wg_ref[...])
        up = _dot(hb, wu_ref[...])
        act = (gate * _sigmoid(gate) * up).astype(BF16)
        o_ref[rows, :] += _dot(act, wd_ref[...])

    @pl.when(f == nf - 1)
    def _():
        for r in range(0, hb_ref.shape[0], chunk):
            rows = slice(r, r + chunk)
            o_ref[rows, :] = _layer_norm(alpha * h_ref[rows, :] + o_ref[rows, :], g_ref[...], b_ref[...])


def _ffn(h, wg, wu, wd, ln_g, ln_b, *, alpha):
    m, d = h.shape
    hidden = wg.shape[1]
    tm, tf = _tile(m, 1024), _tile(hidden, 512)
    assert tm % min(FFN_ROW_CHUNK, tm) == 0
    nf = hidden // tf
    kern = functools.partial(_ffn_kernel, alpha=alpha, nf=nf)
    vec = pl.BlockSpec((1, d), lambda i, f: (0, 0))
    return pl.pallas_call(
        kern,
        grid=(m // tm, nf),
        in_specs=[pl.BlockSpec((tm, d), lambda i, f: (i, 0)),
                  pl.BlockSpec((d, tf), lambda i, f: (0, f)),
                  pl.BlockSpec((d, tf), lambda i, f: (0, f)),
                  pl.BlockSpec((tf, d), lambda i, f: (f, 0)),
                  vec, vec],
        out_specs=pl.BlockSpec((tm, d), lambda i, f: (i, 0)),
        out_shape=jax.ShapeDtypeStruct((m, d), F32),
        scratch_shapes=[pltpu.VMEM((tm, d), BF16)],
        compiler_params=_params("parallel", "arbitrary"),
        name="ffn",
    )(h, wg, wu, wd, ln_g, ln_b)


def kernel(x, w_in, w_proj_a, w_proj_b, w_out, hgrn_norm_g, hgrn_lb_logits, ln1_g, ln1_b,
           w_gate_ffn, w_up_ffn, w_down_ffn, ln2_g, ln2_b):
    bsz, seq, d = x.shape
    depth = w_in.shape[0]
    a_width = w_proj_a.shape[1]
    b_vwidth = w_proj_b.shape[1]
    b_fwidth = hgrn_lb_logits.shape[1]
    assert a_width % HEAD_DIM == 0 and b_vwidth == b_fwidth and b_fwidth % HEAD_DIM == 0
    assert w_in.shape[2] == 3 * a_width + 2 * b_fwidth + 2 * b_vwidth + 2 * d
    a_heads = a_width // HEAD_DIM
    b_heads = b_fwidth // HEAD_DIM
    qa_col, ka_col, va_col = 0, a_heads, 2 * a_heads
    qb_col = 3 * a_heads
    fb_col, ib_col, ogb_col = qb_col + b_heads, qb_col + 2 * b_heads, qb_col + 3 * b_heads
    ga_off = 3 * a_width + 2 * b_fwidth + 2 * b_vwidth
    gb_off = ga_off + d
    alpha = (2.0 * depth) ** 0.25

    lb_all = jnp.cumsum(jax.nn.softmax(hgrn_lb_logits.astype(F32), axis=0), axis=0)

    h = x.reshape(bsz * seq, d)
    for l in range(depth):
        z = _inproj(h, w_in[l].astype(BF16))
        o_a = _moba(z, bsz, seq, a_heads, qa_col, ka_col, va_col)
        lb = lb_all[l].reshape(b_heads, 1, HEAD_DIM)
        gain = hgrn_norm_g[l].astype(F32).reshape(b_heads, 1, HEAD_DIM)
        b_cols = dict(heads=b_heads, q_col=qb_col, f_col=fb_col, i_col=ib_col, og_col=ogb_col)
        mix = functools.partial(_mixout, alpha=alpha, ga_off=ga_off, gb_off=gb_off, nb=seq // MOBA_BLOCK)
        mix_weights = (w_proj_a[l].astype(BF16), w_proj_b[l].astype(BF16), w_out[l].astype(BF16),
                       ln1_g[l].reshape(1, d), ln1_b[l].reshape(1, d))

        def fused(o_a, z, h, lb, gain, *w):
            return mix(o_a, None, z, h, *w, hgrn=dict(lb=lb, gain=gain, **b_cols))

        def split(o_a, z, h, lb, gain, *w):
            o_b = _hgrn_call(z, lb, gain, bsz=bsz, seq=seq, chunk=HGRN_SAFE_CHUNK, sub=1, unroll=False, **b_cols)
            return mix(o_a, o_b, z, h, *w)

        h = lax.cond(_hgrn_fast_ok(lb), fused, split, o_a, z, h, lb, gain, *mix_weights)
        h = _ffn(h, w_gate_ffn[l].astype(BF16), w_up_ffn[l].astype(BF16), w_down_ffn[l].astype(BF16),
                 ln2_g[l].reshape(1, d), ln2_b[l].reshape(1, d), alpha=alpha)
    return h.reshape(bsz, seq, d)
```

```python
import functools
import math

import jax
import jax.numpy as jnp
from jax import lax
from jax.experimental import pallas as pl
from jax.experimental.pallas import tpu as pltpu

F32 = jnp.float32
BF16 = jnp.bfloat16

HEAD_DIM = 128
MOBA_BLOCK = 256
MOBA_TOPK = 3
MOBA_GROUP = 2
MOBA_HEADS_PER_STEP = 2
MOBA_MAX_FLOOR = float(jnp.finfo(jnp.float32).min)
MOBA_ONES_ROWS = 16
LOG2_E = 1.4426950408889634
LN_EPS = 1e-5
RMS_EPS = 1e-6
NEG_INF = float("-inf")

HGRN_CHUNK = 64
HGRN_SUB = 16
HGRN_SAFE_CHUNK = 8
HGRN_MAX_EXPONENT = 60.0

VMEM_LIMIT_BYTES = 56 * 1024 * 1024
FFN_ROW_CHUNK = 512

NT_DIMS = (((1,), (1,)), ((), ()))
TN_DIMS = (((0,), (0,)), ((), ()))


def _tile(n, pref):
    if n <= pref:
        return n
    t = pref - pref % 128
    while t >= 128:
        if n % t == 0:
            return t
        t -= 128
    raise ValueError(f"no 128-aligned tile for {n}")


def _params(*sem):
    return pltpu.CompilerParams(dimension_semantics=sem, vmem_limit_bytes=VMEM_LIMIT_BYTES)


def _aligned(offset, multiple):
    return offset if isinstance(offset, int) else pl.multiple_of(offset, multiple)


def _dot(a, b):
    return jnp.dot(a, b, preferred_element_type=F32)


def _layer_norm(pre, g, b):
    mu = jnp.mean(pre, axis=-1, keepdims=True)
    d = pre - mu
    var = jnp.mean(d * d, axis=-1, keepdims=True)
    return d * lax.rsqrt(var + LN_EPS) * g + b


def _inproj_kernel(x_ref, w_ref, o_ref, xb_ref):
    @pl.when(pl.program_id(1) == 0)
    def _():
        xb_ref[...] = x_ref[...].astype(BF16)

    o_ref[...] = _dot(xb_ref[...], w_ref[...]).astype(o_ref.dtype)


def _inproj(x, w):
    m, k = x.shape
    n = w.shape[1]
    tm, tn = _tile(m, 1024), _tile(n, 1024)
    return pl.pallas_call(
        _inproj_kernel,
        grid=(m // tm, n // tn),
        in_specs=[pl.BlockSpec((tm, k), lambda i, j: (i, 0)),
                  pl.BlockSpec((k, tn), lambda i, j: (0, j))],
        out_specs=pl.BlockSpec((tm, tn), lambda i, j: (i, j)),
        out_shape=jax.ShapeDtypeStruct((m, n), BF16),
        scratch_shapes=[pltpu.VMEM((tm, k), BF16)],
        compiler_params=_params("parallel", "arbitrary"),
        name="inproj",
    )(x, w)


def _moba_kernel(z_hbm, oa_ref, ob_ref, q_ref, k_ref, v_ref, dma_sem, kmean_ref, vt_ref, qts_ref, sel_ref,
                 sa_ref, sb_ref, ma_ref, mb_ref, pa_ref, pb_ref, acc_ref,
                 *, nb, nbp, log2_scale, hp, q_col, k_col, v_col):
    i = pl.program_id(2)
    blk = MOBA_BLOCK
    grp = MOBA_GROUP
    heads = range(hp)
    cols = [slice(h * HEAD_DIM, (h + 1) * HEAD_DIM) for h in heads]

    @pl.when(i == 0)
    def _():
        width = hp * HEAD_DIM
        copies = [
            pltpu.make_async_copy(
                z_hbm.at[pl.program_id(0), :, pl.ds((col // hp + pl.program_id(1)) * width, width)], dst,
                dma_sem.at[idx])
            for idx, (col, dst) in enumerate(((q_col, q_ref), (k_col, k_ref), (v_col, v_ref)))]
        for cp in copies:
            cp.start()
        for cp in copies:
            cp.wait()
        kmean_ref[...] = jnp.zeros_like(kmean_ref)

        def prep(n, carry):
            rows = pl.ds(pl.multiple_of(n * blk, blk), blk)
            for h in heads:
                kf = k_ref[rows, cols[h]].astype(F32)
                kmean_ref[h, pl.ds(n, 1), :] = jnp.sum(kf, axis=0, keepdims=True) * (1.0 / blk)
                vt_ref[h, n, :HEAD_DIM, :] = v_ref[rows, cols[h]].astype(F32).T.astype(BF16)
                vt_ref[h, n, HEAD_DIM:, :] = jnp.ones((MOBA_ONES_ROWS, blk), BF16)
            return carry

        lax.fori_loop(0, nb, prep, 0)
        row = lax.broadcasted_iota(jnp.int32, (nbp, blk), 0)

        def select(n, carry):
            rows = pl.ds(pl.multiple_of(n * blk, blk), blk)
            for h in heads:
                qt32 = q_ref[rows, cols[h]].astype(F32).T
                qt = qt32.astype(BF16)
                qts_ref[h, n] = (qt32 * log2_scale).astype(BF16)
                km = kmean_ref[h]
                km_hi = km.astype(BF16)
                km_lo = (km - km_hi.astype(F32)).astype(BF16)
                gate = _dot(km_hi, qt) + _dot(km_lo, qt)
                g = jnp.where(row < n, gate, NEG_INF)
                sel = jnp.zeros((nbp, blk), dtype=jnp.bool_)
                for _ in range(MOBA_TOPK):
                    mx = jnp.max(g, axis=0, keepdims=True)
                    idx = jnp.min(jnp.where(g == mx, row, nbp), axis=0, keepdims=True)
                    pick = (row == idx) & (mx > NEG_INF)
                    sel = sel | pick
                    g = jnp.where(pick, NEG_INF, g)
                sel_ref[h, n] = sel.astype(F32)
            return carry

        lax.fori_loop(0, nb, select, 0, unroll=2)

    blk_a, blk_b = i, nb - 1 - i
    n_a = (blk_a + grp) // grp
    n_total = nb // grp + 1

    def group(g):
        in_b = g >= n_a
        qb = jnp.where(in_b, blk_b, blk_a)
        n = jnp.where(in_b, n_total - 1 - g, g)
        ids = [qb - n * grp - c for c in range(grp)]
        return dict(qb=qb, slot=in_b.astype(jnp.int32), own=g in (0, n_total - 1),
                    restart=True if g == 0 else g == n_a,
                    js=[jnp.maximum(j, 0) for j in ids], oks=[(j >= 0).astype(F32) for j in ids])

    def scores_into(sbuf, mxbuf, d):
        for h in heads:
            for c, j in enumerate(d["js"]):
                s = _dot(k_ref[pl.ds(pl.multiple_of(j * blk, blk), blk), cols[h]], qts_ref[h, d["qb"]])
                if d["own"] and c == 0:
                    s = jnp.where(lax.broadcasted_iota(jnp.int32, (blk, blk), 0)
                                  <= lax.broadcasted_iota(jnp.int32, (blk, blk), 1), s, NEG_INF)
                sbuf[h, c] = s
                mxbuf[h, c] = jnp.broadcast_to(jnp.max(s, axis=0, keepdims=True), (8, blk))

    def softmax_into(pbuf, sbuf, mxbuf, d, m_old):
        floor = jnp.full((1, blk), MOBA_MAX_FLOOR, F32)
        m_out, alpha_out = [], []
        for h in heads:
            chosen = [None if (d["own"] and c == 0)
                      else sel_ref[h, d["qb"], pl.ds(d["js"][c], 1), :] * d["oks"][c] > 0.0 for c in range(grp)]
            if d["restart"] is True:
                m_prev = floor
            else:
                m_prev = jnp.where(jnp.full((1, blk), d["restart"].astype(F32)) > 0.0, floor, m_old[h])
            m_new = m_prev
            for c in range(grp):
                cm = mxbuf[h, c, 0:1, :]
                m_new = jnp.maximum(m_new, cm if chosen[c] is None else jnp.where(chosen[c], cm, NEG_INF))
            for c in range(grp):
                mb = m_new if chosen[c] is None else jnp.where(chosen[c], m_new, jnp.inf)
                pbuf[h, c] = jnp.exp2(sbuf[h, c] - mb).astype(BF16)
            m_out.append(m_new)
            alpha_out.append(jnp.exp2(m_prev - m_new))
        return m_out, alpha_out

    def accumulate(pbuf, d, alpha):
        for h in heads:
            pv = None
            for c, j in enumerate(d["js"]):
                part = _dot(vt_ref[h, j], pbuf[h, c])
                pv = part if pv is None else pv + part
            acc_ref[d["slot"], h] = alpha[h] * acc_ref[d["slot"], h] + pv

    s_bufs, mx_bufs, p_bufs = (sa_ref, sb_ref), (ma_ref, mb_ref), (pa_ref, pb_ref)
    acc_ref[...] = jnp.zeros_like(acc_ref)
    groups = [group(g) for g in range(n_total)]
    m = [None for _ in heads]
    alpha = None
    scores_into(s_bufs[0], mx_bufs[0], groups[0])
    for g in range(n_total):
        if g + 1 < n_total:
            scores_into(s_bufs[(g + 1) % 2], mx_bufs[(g + 1) % 2], groups[g + 1])
        if g > 0:
            accumulate(p_bufs[(g - 1) % 2], groups[g - 1], alpha)
        m, alpha = softmax_into(p_bufs[g % 2], s_bufs[g % 2], mx_bufs[g % 2], groups[g], m)
    accumulate(p_bufs[(n_total - 1) % 2], groups[n_total - 1], alpha)
    for slot, out_ref in enumerate((oa_ref, ob_ref)):
        for h in heads:
            num = acc_ref[slot, h, :HEAD_DIM, :]
            den = acc_ref[slot, h, HEAD_DIM:HEAD_DIM + 1, :]
            out_ref[cols[h], :] = (num / den).astype(out_ref.dtype)


def _moba(z, bsz, seq, heads, q_col, k_col, v_col):
    blk = MOBA_BLOCK
    assert seq % (2 * blk) == 0 and MOBA_GROUP == 2
    nb = seq // blk
    half = nb // 2
    nbp = -(-nb // 8) * 8
    acc_rows = HEAD_DIM + MOBA_ONES_ROWS
    hp = math.gcd(heads, MOBA_HEADS_PER_STEP)
    assert all(col % hp == 0 for col in (q_col, k_col, v_col))
    width = hp * HEAD_DIM
    z3 = z.reshape(bsz, seq, z.shape[1])
    kern = functools.partial(_moba_kernel, nb=nb, nbp=nbp, log2_scale=HEAD_DIM ** -0.5 * LOG2_E, hp=hp,
                             q_col=q_col, k_col=k_col, v_col=v_col)
    seq_buf = pltpu.VMEM((seq, width), BF16)
    return pl.pallas_call(
        kern,
        grid=(bsz, heads // hp, half),
        in_specs=[pl.BlockSpec(memory_space=pl.ANY)],
        out_specs=[pl.BlockSpec((width, blk), lambda b, h, i: (h, b * half + i))] * 2,
        out_shape=[jax.ShapeDtypeStruct((heads * HEAD_DIM, bsz * seq // 2), BF16)] * 2,
        scratch_shapes=[seq_buf, seq_buf, seq_buf,
                        pltpu.SemaphoreType.DMA((3,)),
                        pltpu.VMEM((hp, nbp, HEAD_DIM), F32),
                        pltpu.VMEM((hp, nb, acc_rows, blk), BF16),
                        pltpu.VMEM((hp, nb, HEAD_DIM, blk), BF16),
                        pltpu.VMEM((hp, nb, nbp, blk), F32),
                        pltpu.VMEM((hp, MOBA_GROUP, blk, blk), F32),
                        pltpu.VMEM((hp, MOBA_GROUP, blk, blk), F32),
                        pltpu.VMEM((hp, MOBA_GROUP, 8, blk), F32),
                        pltpu.VMEM((hp, MOBA_GROUP, 8, blk), F32),
                        pltpu.VMEM((hp, MOBA_GROUP, blk, blk), BF16),
                        pltpu.VMEM((hp, MOBA_GROUP, blk, blk), BF16),
                        pltpu.VMEM((2, hp, acc_rows, blk), F32)],
        compiler_params=_params("parallel", "parallel", "arbitrary"),
        name="moba",
    )(z3)


def _hgrn_kernel(q_ref, f_ref, i_ref, og_ref, lb_ref, gain_ref, o_ref, st_ref, *, tt, chunk, sub, unroll):
    @pl.when(pl.program_id(2) == 0)
    def _():
        st_ref[...] = jnp.zeros_like(st_ref)

    n_blocks = chunk // sub
    stack = n_blocks * chunk
    lb = lb_ref[...]
    gain = gain_ref[...]
    tri = (lax.broadcasted_iota(jnp.int32, (chunk, chunk), 0)
           >= lax.broadcasted_iota(jnp.int32, (chunk, chunk), 1)).astype(BF16)
    r_i = lax.broadcasted_iota(jnp.int32, (chunk, stack), 0)
    c_i = lax.broadcasted_iota(jnp.int32, (chunk, stack), 1)
    keep = ((c_i // chunk) == (r_i // sub)) & ((c_i % chunk) <= r_i)
    row_c = lax.broadcasted_iota(jnp.int32, (chunk, HEAD_DIM), 0)

    def one_chunk(n, carry):
        off = pl.multiple_of(n * chunk, chunk)
        xq = q_ref[pl.ds(off, chunk), :].astype(F32)
        qs = xq * jax.nn.sigmoid(xq)
        f = lb + (1.0 - lb) * jax.nn.sigmoid(f_ref[pl.ds(off, chunk), :].astype(F32))
        logf = jnp.log(f)
        kk = 1.0 - f
        vb = i_ref[pl.ds(off, chunk), :]

        hi = logf.astype(BF16)
        r1 = logf - hi.astype(F32)
        mid = r1.astype(BF16)
        lo = (r1 - mid.astype(F32)).astype(BF16)
        b = _dot(tri, hi) + _dot(tri, mid) + _dot(tri, lo)
        b_last = b[chunk - 1:chunk, :]

        st = st_ref[...]
        qe = (qs * jnp.exp(b)).astype(BF16)
        o_inter = lax.dot_general(qe, st.astype(BF16), NT_DIMS, preferred_element_type=F32)

        ref_rows = jnp.concatenate(
            [jnp.broadcast_to(b[blk * sub:blk * sub + 1, :], (sub, HEAD_DIM)) for blk in range(n_blocks)],
            axis=0)
        qt = (qs * jnp.exp(b - ref_rows)).astype(BF16)
        kh = jnp.concatenate(
            [(kk * jnp.exp(jnp.where(row_c < (blk + 1) * sub, b[blk * sub:blk * sub + 1, :] - b, NEG_INF))
              ).astype(BF16) for blk in range(n_blocks)], axis=0)
        res = lax.dot_general(qt, kh, NT_DIMS, preferred_element_type=F32)
        a = jnp.where(keep, res, 0.0).astype(BF16)
        o = o_inter + _dot(a, jnp.concatenate([vb] * n_blocks, axis=0))

        kd = (kk * jnp.exp(b_last - b)).astype(BF16)
        st_ref[...] = st * jnp.exp(b_last) + lax.dot_general(vb, kd, TN_DIMS, preferred_element_type=F32)

        ms = jnp.mean(o * o, axis=-1, keepdims=True)
        on = o * lax.rsqrt(ms + RMS_EPS) * gain
        og = og_ref[pl.ds(off, chunk), :].astype(F32)
        o_ref[pl.ds(off, chunk), :] = (on * jax.nn.sigmoid(og)).astype(o_ref.dtype)
        return carry

    lax.fori_loop(0, tt // chunk, one_chunk, 0, unroll=unroll)


def _sigmoid(x):
    return 0.5 * jnp.tanh(0.5 * x) + 0.5


def _rows_to_blocks(rows, sub):
    return jnp.concatenate([jnp.broadcast_to(rows[k:k + 1, :], (sub, HEAD_DIM)) for k in range(rows.shape[0])],
                           axis=0)


def _hgrn_spans(q_ref, f_ref, i_ref, og_ref, lb_ref, gain_ref, o_ref, st_ref, *, n_spans, span, chunk, sub, hp,
                hooks=()):
    n_chunks = span // chunk
    nblk = chunk // sub
    n_all = span // sub
    stack = nblk * chunk
    r_s = lax.broadcasted_iota(jnp.int32, (span, span), 0)
    c_s = lax.broadcasted_iota(jnp.int32, (span, span), 1)
    tri = ((r_s >= c_s) & (r_s // chunk == c_s // chunk)).astype(BF16)
    r_i = lax.broadcasted_iota(jnp.int32, (chunk, stack), 0)
    c_i = lax.broadcasted_iota(jnp.int32, (chunk, stack), 1)
    keep = ((c_i // chunk) == (r_i // sub)) & ((c_i % chunk) <= r_i)
    pair = lax.broadcasted_iota(jnp.int32, (nblk * nblk, HEAD_DIM), 0)
    pair_ok = (pair % nblk) <= (pair // nblk)
    heads = range(hp)
    cols = [slice(h * HEAD_DIM, (h + 1) * HEAD_DIM) for h in heads]
    f_mid = [0.5 + 0.5 * lb_ref[h] for h in heads]
    f_amp = [0.5 - 0.5 * lb_ref[h] for h in heads]

    def hook(k):
        if k < len(hooks):
            hooks[k]()

    def one_span(sidx, carry):
        rows = pl.ds(_aligned(sidx * span, span), span)

        qs, kk, vb, b = [], [], [], []
        for h in heads:
            xq = q_ref[rows, cols[h]].astype(F32)
            qs.append(xq * _sigmoid(xq))
            th = jnp.tanh(0.5 * f_ref[rows, cols[h]].astype(F32))
            logf = jnp.log(f_mid[h] + f_amp[h] * th)
            kk.append(f_amp[h] - f_amp[h] * th)
            vb.append(i_ref[rows, cols[h]])
            hi = logf.astype(BF16)
            r1 = logf - hi.astype(F32)
            mid = r1.astype(BF16)
            lo = (r1 - mid.astype(F32)).astype(BF16)
            b.append(_dot(tri, hi) + _dot(tri, mid) + _dot(tri, lo))

        hook(0)
        qe, b_last, res, upd = [], [], [], []
        for h in heads:
            first = jnp.concatenate([b[h][k * sub:k * sub + 1, :] for k in range(n_all)], axis=0)
            last = jnp.concatenate([b[h][k * sub + sub - 1:(k + 1) * sub, :] for k in range(n_all)], axis=0)
            qt = qs[h] * jnp.exp(b[h] - _rows_to_blocks(first, sub))
            kt = kk[h] * jnp.exp(_rows_to_blocks(last, sub) - b[h])
            bl = [last[(c + 1) * nblk - 1:(c + 1) * nblk, :] for c in range(n_chunks)]
            to_end = jnp.concatenate([bl[c] - last[c * nblk:(c + 1) * nblk, :] for c in range(n_chunks)], axis=0)
            qe.append((qt * _rows_to_blocks(jnp.exp(first), sub)).astype(BF16))
            kd = (kt * _rows_to_blocks(jnp.exp(to_end), sub)).astype(BF16)
            qtb = qt.astype(BF16)
            b_last.append(bl)
            res_h, upd_h = [], []
            for c in range(n_chunks):
                lo_r, hi_r = c * chunk, (c + 1) * chunk
                fr = first[c * nblk:(c + 1) * nblk, :]
                la = last[c * nblk:(c + 1) * nblk, :]
                g = jnp.where(pair_ok,
                              jnp.exp(_rows_to_blocks(fr, nblk) - jnp.concatenate([la] * nblk, axis=0)), 0.0)
                ktc = kt[lo_r:hi_r, :]
                kh = jnp.concatenate(
                    [ktc[jb * sub:(jb + 1) * sub, :] * g[ib * nblk + jb:ib * nblk + jb + 1, :]
                     for ib in range(nblk) for jb in range(nblk)], axis=0).astype(BF16)
                res_h.append(lax.dot_general(qtb[lo_r:hi_r, :], kh, NT_DIMS, preferred_element_type=F32))
                upd_h.append(lax.dot_general(vb[h][lo_r:hi_r, :], kd[lo_r:hi_r, :], TN_DIMS,
                                             preferred_element_type=F32))
            res.append(res_h)
            upd.append(upd_h)

        hook(1)
        intra = [[_dot(jnp.where(keep, res[h][c], 0.0).astype(BF16),
                       jnp.concatenate([vb[h][c * chunk:(c + 1) * chunk, :]] * nblk, axis=0))
                  for c in range(n_chunks)] for h in heads]

        hook(2)
        st = [st_ref[h] for h in heads]
        for c in range(n_chunks):
            lo_r, hi_r = c * chunk, (c + 1) * chunk
            out_rows = pl.ds(_aligned(sidx * span + lo_r, chunk), chunk)
            for h in heads:
                o = intra[h][c] + lax.dot_general(qe[h][lo_r:hi_r, :], st[h].astype(BF16), NT_DIMS,
                                                  preferred_element_type=F32)
                st[h] = st[h] * jnp.exp(b_last[h][c]) + upd[h][c]
                ms = jnp.mean(o * o, axis=-1, keepdims=True)
                on = o * lax.rsqrt(ms + RMS_EPS) * gain_ref[h]
                og = og_ref[out_rows, cols[h]].astype(F32)
                o_ref[out_rows, cols[h]] = (on * _sigmoid(og)).astype(o_ref.dtype)
            hook(3 + c)
        for h in heads:
            st_ref[h] = st[h]
        return carry

    if n_spans == 1:
        one_span(0, 0)
    else:
        lax.fori_loop(0, n_spans, one_span, 0)


def _hgrn_safe(z, lb, gain, *, bsz, seq, heads, q_col, f_col, i_col, og_col):
    tt = _tile(seq, 256)
    nt = seq // tt
    kern = functools.partial(_hgrn_kernel, tt=tt, chunk=HGRN_SAFE_CHUNK, sub=1, unroll=False)

    def col_spec(col):
        return pl.BlockSpec((tt, HEAD_DIM), lambda b, h, t: (b * nt + t, col + h))

    head_spec = pl.BlockSpec((None, 1, HEAD_DIM), lambda b, h, t: (h, 0, 0))
    return pl.pallas_call(
        kern,
        grid=(bsz, heads, nt),
        in_specs=[col_spec(q_col), col_spec(f_col), col_spec(i_col), col_spec(og_col), head_spec, head_spec],
        out_specs=pl.BlockSpec((tt, HEAD_DIM), lambda b, h, t: (b * nt + t, h)),
        out_shape=jax.ShapeDtypeStruct((bsz * seq, heads * HEAD_DIM), BF16),
        scratch_shapes=[pltpu.VMEM((HEAD_DIM, HEAD_DIM), F32)],
        compiler_params=_params("parallel", "parallel", "arbitrary"),
        name="hgrn_safe",
    )(z, z, z, z, lb, gain)


def _hgrn_fast_ok(lb):
    return -(HGRN_SUB - 1) * jnp.log(jnp.min(lb)) <= HGRN_MAX_EXPONENT


def _mixout_kernel(*refs, alpha, n_gate, gw, nb, hgrn):
    if hgrn:
        (q_ref, f_ref, i_ref, og_ref, lb_ref, gain_ref), refs = refs[:6], refs[6:]
    (oa_lo_ref, oa_hi_ref), refs = refs[:2], refs[2:]
    if not hgrn:
        ob_in_ref, refs = refs[0], refs[1:]
    ga_refs, gb_refs, refs = refs[:n_gate], refs[n_gate:2 * n_gate], refs[2 * n_gate:]
    x_ref, wpa_ref, wpb_ref, wo_ref, g_ref, b_ref, o_ref = refs[:7]
    step = pl.program_id(0)
    if hgrn:
        st_ref, ob_buf_ref = refs[7:]
        tile = jnp.maximum(step - 1, 0)
        next_tile = jnp.minimum(step, pl.num_programs(0) - 2)
        slot = step % 2

        @pl.when(step == 0)
        def _():
            ob_buf_ref[...] = jnp.zeros_like(ob_buf_ref)

        @pl.when(next_tile % nb == 0)
        def _():
            st_ref[...] = jnp.zeros_like(st_ref)

        ob = ob_buf_ref[1 - slot]
    else:
        tile = step
        ob = ob_in_ref[...]
    in_lo = tile % nb < nb // 2
    oa_t = jnp.where(in_lo, oa_lo_ref[...], oa_hi_ref[...])
    merged, parts = {}, []

    def project(c):
        ya = lax.dot_general(oa_t, wpa_ref[:, c * gw:(c + 1) * gw], TN_DIMS, preferred_element_type=F32)
        yb = _dot(ob, wpb_ref[:, c * gw:(c + 1) * gw])
        merged[c] = (jax.nn.sigmoid(ga_refs[c][...].astype(F32)) * ya
                     + jax.nn.sigmoid(gb_refs[c][...].astype(F32)) * yb).astype(BF16)

    def output(c):
        parts.append(_dot(merged[c], wo_ref[c * gw:(c + 1) * gw, :]))

    def finish():
        o_ref[...] = _layer_norm(alpha * x_ref[...] + sum(parts[1:], parts[0]), g_ref[...], b_ref[...])

    def chain(*fs):
        return lambda: [f() for f in fs]

    steps = [functools.partial(project, 0)]
    steps += [chain(functools.partial(output, c - 1), functools.partial(project, c)) for c in range(1, n_gate)]
    pieces = steps + [functools.partial(output, n_gate - 1), finish]
    if hgrn:
        n_hooks = 3 + ob_buf_ref.shape[1] // hgrn["chunk"]
        _hgrn_spans(q_ref, f_ref, i_ref, og_ref, lb_ref, gain_ref, ob_buf_ref.at[slot], st_ref, n_spans=1,
                    span=ob_buf_ref.shape[1], hooks=pieces[:n_hooks], **hgrn)
        pieces = pieces[n_hooks:]
    for piece in pieces:
        piece()


def _mixout(o_a_halves, o_b, z, x, wpa, wpb, wo, ln_g, ln_b, *, alpha, ga_off, gb_off, nb, hgrn=None):
    m, d = x.shape
    gw = math.gcd(math.gcd(ga_off, gb_off), d)
    n_gate = d // gw
    tm = MOBA_BLOCK
    half = nb // 2
    n_tiles = m // tm
    if hgrn:
        hb = hgrn["heads"]
        b_width = hb * HEAD_DIM
        assert tm % HGRN_CHUNK == 0 and all(hgrn[k] % hb == 0 for k in ("q_col", "f_col", "i_col", "og_col"))
        mix_tile = lambda i: jnp.maximum(i - 1, 0)
        next_tile = lambda i: jnp.minimum(i, n_tiles - 1)
        settings = dict(chunk=HGRN_CHUNK, sub=HGRN_SUB, hp=hb)
    else:
        mix_tile = lambda i: i
        settings = None
    kern = functools.partial(_mixout_kernel, alpha=alpha, n_gate=n_gate, gw=gw, nb=nb, hgrn=settings)

    def gate_spec(off, c):
        return pl.BlockSpec((tm, gw), lambda i: (mix_tile(i), off // gw + c))

    def whole(a):
        return pl.BlockSpec(a.shape, lambda i: (0,) * a.ndim, pipeline_mode=pl.Buffered(1))

    def rows(a):
        return pl.BlockSpec((tm, a.shape[1]), lambda i: (mix_tile(i), 0))

    def attn_spec(pos):
        def index(i):
            t = mix_tile(i)
            return 0, (t // nb) * half + pos(t % nb)
        return pl.BlockSpec((o_a_halves[0].shape[0], tm), index)

    attn_specs = [attn_spec(lambda blk: jnp.minimum(blk, half - 1)),
                  attn_spec(lambda blk: nb - 1 - jnp.maximum(blk, half))]
    tail_specs = ([gate_spec(ga_off, c) for c in range(n_gate)] + [gate_spec(gb_off, c) for c in range(n_gate)]
                  + [rows(x), whole(wpa), whole(wpb), whole(wo), whole(ln_g), whole(ln_b)])
    tail_args = (*([z] * (2 * n_gate)), x, wpa, wpb, wo, ln_g, ln_b)
    if hgrn:
        def col_spec(col):
            return pl.BlockSpec((tm, b_width), lambda i: (next_tile(i), col // hb))
        in_specs = ([col_spec(hgrn[k]) for k in ("q_col", "f_col", "i_col", "og_col")]
                    + [whole(hgrn["lb"]), whole(hgrn["gain"])] + attn_specs + tail_specs)
        args = (z, z, z, z, hgrn["lb"], hgrn["gain"], *o_a_halves, *tail_args)
        scratch = [pltpu.VMEM((hb, HEAD_DIM, HEAD_DIM), F32),
                   pltpu.VMEM((2, tm, b_width), BF16)]
    else:
        in_specs = attn_specs + [rows(o_b)] + tail_specs
        args = (*o_a_halves, o_b, *tail_args)
        scratch = []
    return pl.pallas_call(
        kern,
        grid=(n_tiles + (1 if hgrn else 0),),
        in_specs=in_specs,
        out_specs=pl.BlockSpec((tm, d), lambda i: (mix_tile(i), 0)),
        out_shape=jax.ShapeDtypeStruct((m, d), F32),
        scratch_shapes=scratch,
        compiler_params=_params("arbitrary"),
        name="hgrn_mixout" if hgrn else "mixout",
    )(*args)


def _ffn_kernel(h_ref, wg_ref, wu_ref, wd_ref, g_ref, b_ref, o_ref, hb_ref, *, alpha, nf):
    f = pl.program_id(1)
    chunk = min(FFN_ROW_CHUNK, hb_ref.shape[0])

    def step(first, last):
        for r in range(0, hb_ref.shape[0], chunk):
            rows = slice(r, r + chunk)
            if first:
                hb = h_ref[rows, :].astype(BF16)
                hb_ref[rows, :] = hb
            else:
                hb = hb_ref[rows, :]
            gate = _dot(hb, wg_ref[...])
            up = _dot(hb, wu_ref[...])
            act = (gate * _sigmoid(gate) * up).astype(BF16)
            down = _dot(act, wd_ref[...])
            acc = down if first else o_ref[rows, :] + down
            o_ref[rows, :] = _layer_norm(alpha * h_ref[rows, :] + acc, g_ref[...], b_ref[...]) if last else acc

    if nf == 1:
        step(True, True)
    else:
        pl.when(f == 0)(functools.partial(step, True, False))
        pl.when(f == nf - 1)(functools.partial(step, False, True))
        if nf > 2:
            pl.when((f > 0) & (f < nf - 1))(functools.partial(step, False, False))


def _ffn(h, wg, wu, wd, ln_g, ln_b, *, alpha):
    m, d = h.shape
    hidden = wg.shape[1]
    tm, tf = _tile(m, 1024), _tile(hidden, 512)
    assert tm % min(FFN_ROW_CHUNK, tm) == 0
    nf = hidden // tf
    kern = functools.partial(_ffn_kernel, alpha=alpha, nf=nf)
    vec = pl.BlockSpec((1, d), lambda i, f: (0, 0))
    return pl.pallas_call(
        kern,
        grid=(m // tm, nf),
        in_specs=[pl.BlockSpec((tm, d), lambda i, f: (i, 0)),
                  pl.BlockSpec((d, tf), lambda i, f: (0, f)),
                  pl.BlockSpec((d, tf), lambda i, f: (0, f)),
                  pl.BlockSpec((tf, d), lambda i, f: (f, 0)),
                  vec, vec],
        out_specs=pl.BlockSpec((tm, d), lambda i, f: (i, 0)),
        out_shape=jax.ShapeDtypeStruct((m, d), F32),
        scratch_shapes=[pltpu.VMEM((tm, d), BF16)],
        compiler_params=_params("parallel", "arbitrary"),
        name="ffn",
    )(h, wg, wu, wd, ln_g, ln_b)


def kernel(x, w_in, w_proj_a, w_proj_b, w_out, hgrn_norm_g, hgrn_lb_logits, ln1_g, ln1_b,
           w_gate_ffn, w_up_ffn, w_down_ffn, ln2_g, ln2_b):
    bsz, seq, d = x.shape
    depth = w_in.shape[0]
    a_width = w_proj_a.shape[1]
    b_vwidth = w_proj_b.shape[1]
    b_fwidth = hgrn_lb_logits.shape[1]
    assert a_width % HEAD_DIM == 0 and b_vwidth == b_fwidth and b_fwidth % HEAD_DIM == 0
    assert w_in.shape[2] == 3 * a_width + 2 * b_fwidth + 2 * b_vwidth + 2 * d
    a_heads = a_width // HEAD_DIM
    b_heads = b_fwidth // HEAD_DIM
    qa_col, ka_col, va_col = 0, a_heads, 2 * a_heads
    qb_col = 3 * a_heads
    fb_col, ib_col, ogb_col = qb_col + b_heads, qb_col + 2 * b_heads, qb_col + 3 * b_heads
    ga_off = 3 * a_width + 2 * b_fwidth + 2 * b_vwidth
    gb_off = ga_off + d
    alpha = (2.0 * depth) ** 0.25

    lb_all = jnp.cumsum(jax.nn.softmax(hgrn_lb_logits.astype(F32), axis=0), axis=0)

    h = x.reshape(bsz * seq, d)
    for l in range(depth):
        z = _inproj(h, w_in[l].astype(BF16))
        o_a = _moba(z, bsz, seq, a_heads, qa_col, ka_col, va_col)
        lb = lb_all[l].reshape(b_heads, 1, HEAD_DIM)
        gain = hgrn_norm_g[l].astype(F32).reshape(b_heads, 1, HEAD_DIM)
        b_cols = dict(heads=b_heads, q_col=qb_col, f_col=fb_col, i_col=ib_col, og_col=ogb_col)
        mix = functools.partial(_mixout, alpha=alpha, ga_off=ga_off, gb_off=gb_off, nb=seq // MOBA_BLOCK)
        mix_weights = (w_proj_a[l].astype(BF16), w_proj_b[l].astype(BF16), w_out[l].astype(BF16),
                       ln1_g[l].reshape(1, d), ln1_b[l].reshape(1, d))

        def fused(o_a, z, h, lb, gain, *w):
            return mix(o_a, None, z, h, *w, hgrn=dict(lb=lb, gain=gain, **b_cols))

        def split(o_a, z, h, lb, gain, *w):
            o_b = _hgrn_safe(z, lb, gain, bsz=bsz, seq=seq, **b_cols)
            return mix(o_a, o_b, z, h, *w)

        h = lax.cond(_hgrn_fast_ok(lb), fused, split, o_a, z, h, lb, gain, *mix_weights)
        h = _ffn(h, w_gate_ffn[l].astype(BF16), w_up_ffn[l].astype(BF16), w_down_ffn[l].astype(BF16),
                 ln2_g[l].reshape(1, d), ln2_b[l].reshape(1, d), alpha=alpha)
    return h.reshape(bsz, seq, d)
```

```python
import functools
import math

import jax
import jax.numpy as jnp
from jax import lax
from jax.experimental import pallas as pl
from jax.experimental.pallas import tpu as pltpu

F32 = jnp.float32
BF16 = jnp.bfloat16

LANES = 128
SUBLANES = 8
HEAD_DIM = 128
MOBA_BLOCK = 256
MOBA_TOPK = 3
MOBA_GROUP = 2
MOBA_HEADS_PER_STEP = 2
MOBA_MAX_FLOOR = float(jnp.finfo(jnp.float32).min)
MOBA_ONES_ROWS = 16
LOG2_E = 1.4426950408889634
LN_EPS = 1e-5
RMS_EPS = 1e-6
NEG_INF = float("-inf")

HGRN_CHUNK = 64
HGRN_SUB = 16
HGRN_SAFE_CHUNK = 8
HGRN_MAX_EXPONENT = 60.0

VMEM_LIMIT_BYTES = 56 * 1024 * 1024
FFN_ROW_CHUNK = 512

NT_DIMS = (((1,), (1,)), ((), ()))
TN_DIMS = (((0,), (0,)), ((), ()))


def _tile(n, pref):
    if n <= pref:
        return n
    t = pref - pref % LANES
    while t >= LANES:
        if n % t == 0:
            return t
        t -= LANES
    raise ValueError(f"no lane-aligned tile for {n}")


def _params(*sem):
    return pltpu.CompilerParams(dimension_semantics=sem, vmem_limit_bytes=VMEM_LIMIT_BYTES)


def _aligned(offset, multiple):
    return offset if isinstance(offset, int) else pl.multiple_of(offset, multiple)


def _dot(a, b):
    return jnp.dot(a, b, preferred_element_type=F32)


def _layer_norm(pre, g, b):
    mu = jnp.mean(pre, axis=-1, keepdims=True)
    d = pre - mu
    var = jnp.mean(d * d, axis=-1, keepdims=True)
    return d * lax.rsqrt(var + LN_EPS) * g + b


def _inproj_kernel(x_ref, w_ref, o_ref, xb_ref):
    @pl.when(pl.program_id(1) == 0)
    def _():
        xb_ref[...] = x_ref[...].astype(BF16)

    o_ref[...] = _dot(xb_ref[...], w_ref[...]).astype(o_ref.dtype)


def _inproj(x, w):
    m, k = x.shape
    n = w.shape[1]
    tm, tn = _tile(m, 1024), _tile(n, 1024)
    return pl.pallas_call(
        _inproj_kernel,
        grid=(m // tm, n // tn),
        in_specs=[pl.BlockSpec((tm, k), lambda i, j: (i, 0)),
                  pl.BlockSpec((k, tn), lambda i, j: (0, j))],
        out_specs=pl.BlockSpec((tm, tn), lambda i, j: (i, j)),
        out_shape=jax.ShapeDtypeStruct((m, n), BF16),
        scratch_shapes=[pltpu.VMEM((tm, k), BF16)],
        compiler_params=_params("parallel", "arbitrary"),
        name="inproj",
    )(x, w)


def _moba_kernel(z_hbm, oa_ref, ob_ref, q_ref, k_ref, v_ref, dma_sem, kmean_ref, vt_ref, qts_ref, sel_ref,
                 sa_ref, sb_ref, ma_ref, mb_ref, pa_ref, pb_ref, acc_ref,
                 *, nb, nbp, log2_scale, hp, q_col, k_col, v_col):
    i = pl.program_id(2)
    blk = MOBA_BLOCK
    grp = MOBA_GROUP
    heads = range(hp)
    cols = [slice(h * HEAD_DIM, (h + 1) * HEAD_DIM) for h in heads]

    @pl.when(i == 0)
    def _():
        width = hp * HEAD_DIM
        copies = [
            pltpu.make_async_copy(
                z_hbm.at[pl.program_id(0), :, pl.ds((col // hp + pl.program_id(1)) * width, width)], dst,
                dma_sem.at[idx])
            for idx, (col, dst) in enumerate(((q_col, q_ref), (k_col, k_ref), (v_col, v_ref)))]
        for cp in copies:
            cp.start()
        for cp in copies:
            cp.wait()
        kmean_ref[...] = jnp.zeros_like(kmean_ref)

        def prep(n, carry):
            rows = pl.ds(pl.multiple_of(n * blk, blk), blk)
            for h in heads:
                kf = k_ref[rows, cols[h]].astype(F32)
                kmean_ref[h, pl.ds(n, 1), :] = jnp.sum(kf, axis=0, keepdims=True) * (1.0 / blk)
                vt_ref[h, n, :HEAD_DIM, :] = v_ref[rows, cols[h]].astype(F32).T.astype(BF16)
                vt_ref[h, n, HEAD_DIM:, :] = jnp.ones((MOBA_ONES_ROWS, blk), BF16)
            return carry

        lax.fori_loop(0, nb, prep, 0)
        row = lax.broadcasted_iota(jnp.int32, (nbp, blk), 0)

        def select(n, carry):
            rows = pl.ds(pl.multiple_of(n * blk, blk), blk)
            for h in heads:
                qt32 = q_ref[rows, cols[h]].astype(F32).T
                qt = qt32.astype(BF16)
                qts_ref[h, n] = (qt32 * log2_scale).astype(BF16)
                km = kmean_ref[h]
                km_hi = km.astype(BF16)
                km_lo = (km - km_hi.astype(F32)).astype(BF16)
                gate = _dot(km_hi, qt) + _dot(km_lo, qt)
                g = jnp.where(row < n, gate, NEG_INF)
                sel = jnp.zeros((nbp, blk), dtype=jnp.bool_)
                for _ in range(MOBA_TOPK):
                    mx = jnp.max(g, axis=0, keepdims=True)
                    idx = jnp.min(jnp.where(g == mx, row, nbp), axis=0, keepdims=True)
                    pick = (row == idx) & (mx > NEG_INF)
                    sel = sel | pick
                    g = jnp.where(pick, NEG_INF, g)
                sel_ref[h, n] = sel.astype(F32)
            return carry

        lax.fori_loop(0, nb, select, 0, unroll=2)

    blk_a, blk_b = i, nb - 1 - i
    n_a = (blk_a + grp) // grp
    n_total = nb // grp + 1

    def group(g):
        in_b = g >= n_a
        qb = jnp.where(in_b, blk_b, blk_a)
        n = jnp.where(in_b, n_total - 1 - g, g)
        ids = [qb - n * grp - c for c in range(grp)]
        return dict(qb=qb, slot=in_b.astype(jnp.int32), own=g in (0, n_total - 1),
                    restart=True if g == 0 else g == n_a,
                    js=[jnp.maximum(j, 0) for j in ids], oks=[(j >= 0).astype(F32) for j in ids])

    def scores_into(sbuf, mxbuf, d):
        for h in heads:
            for c, j in enumerate(d["js"]):
                s = _dot(k_ref[pl.ds(pl.multiple_of(j * blk, blk), blk), cols[h]], qts_ref[h, d["qb"]])
                if d["own"] and c == 0:
                    s = jnp.where(lax.broadcasted_iota(jnp.int32, (blk, blk), 0)
                                  <= lax.broadcasted_iota(jnp.int32, (blk, blk), 1), s, NEG_INF)
                sbuf[h, c] = s
                mxbuf[h, c] = jnp.broadcast_to(jnp.max(s, axis=0, keepdims=True), (SUBLANES, blk))

    def softmax_into(pbuf, sbuf, mxbuf, d, m_old):
        floor = jnp.full((1, blk), MOBA_MAX_FLOOR, F32)
        m_out, alpha_out = [], []
        for h in heads:
            chosen = [None if (d["own"] and c == 0)
                      else sel_ref[h, d["qb"], pl.ds(d["js"][c], 1), :] * d["oks"][c] > 0.0 for c in range(grp)]
            if d["restart"] is True:
                m_prev = floor
            else:
                m_prev = jnp.where(jnp.full((1, blk), d["restart"].astype(F32)) > 0.0, floor, m_old[h])
            m_new = m_prev
            for c in range(grp):
                cm = mxbuf[h, c, 0:1, :]
                m_new = jnp.maximum(m_new, cm if chosen[c] is None else jnp.where(chosen[c], cm, NEG_INF))
            for c in range(grp):
                mb = m_new if chosen[c] is None else jnp.where(chosen[c], m_new, jnp.inf)
                pbuf[h, c] = jnp.exp2(sbuf[h, c] - mb).astype(BF16)
            m_out.append(m_new)
            alpha_out.append(jnp.exp2(m_prev - m_new))
        return m_out, alpha_out

    def accumulate(pbuf, d, alpha):
        for h in heads:
            pv = None
            for c, j in enumerate(d["js"]):
                part = _dot(vt_ref[h, j], pbuf[h, c])
                pv = part if pv is None else pv + part
            acc_ref[d["slot"], h] = alpha[h] * acc_ref[d["slot"], h] + pv

    s_bufs, mx_bufs, p_bufs = (sa_ref, sb_ref), (ma_ref, mb_ref), (pa_ref, pb_ref)
    acc_ref[...] = jnp.zeros_like(acc_ref)
    groups = [group(g) for g in range(n_total)]
    m = [None for _ in heads]
    alpha = None
    scores_into(s_bufs[0], mx_bufs[0], groups[0])
    for g in range(n_total):
        if g + 1 < n_total:
            scores_into(s_bufs[(g + 1) % 2], mx_bufs[(g + 1) % 2], groups[g + 1])
        if g > 0:
            accumulate(p_bufs[(g - 1) % 2], groups[g - 1], alpha)
        m, alpha = softmax_into(p_bufs[g % 2], s_bufs[g % 2], mx_bufs[g % 2], groups[g], m)
    accumulate(p_bufs[(n_total - 1) % 2], groups[n_total - 1], alpha)
    for slot, out_ref in enumerate((oa_ref, ob_ref)):
        for h in heads:
            num = acc_ref[slot, h, :HEAD_DIM, :]
            den = acc_ref[slot, h, HEAD_DIM:HEAD_DIM + 1, :]
            out_ref[cols[h], :] = (num / den).astype(out_ref.dtype)


def _moba(z, bsz, seq, heads, q_col, k_col, v_col):
    blk = MOBA_BLOCK
    assert seq % (2 * blk) == 0 and MOBA_GROUP == 2
    nb = seq // blk
    half = nb // 2
    nbp = -(-nb // SUBLANES) * SUBLANES
    acc_rows = HEAD_DIM + MOBA_ONES_ROWS
    hp = math.gcd(heads, MOBA_HEADS_PER_STEP)
    assert all(col % hp == 0 for col in (q_col, k_col, v_col))
    width = hp * HEAD_DIM
    z3 = z.reshape(bsz, seq, z.shape[1])
    kern = functools.partial(_moba_kernel, nb=nb, nbp=nbp, log2_scale=HEAD_DIM ** -0.5 * LOG2_E, hp=hp,
                             q_col=q_col, k_col=k_col, v_col=v_col)
    seq_buf = pltpu.VMEM((seq, width), BF16)
    return pl.pallas_call(
        kern,
        grid=(bsz, heads // hp, half),
        in_specs=[pl.BlockSpec(memory_space=pl.ANY)],
        out_specs=[pl.BlockSpec((width, blk), lambda b, h, i: (h, b * half + i))] * 2,
        out_shape=[jax.ShapeDtypeStruct((heads * HEAD_DIM, bsz * seq // 2), BF16)] * 2,
        scratch_shapes=[seq_buf, seq_buf, seq_buf,
                        pltpu.SemaphoreType.DMA((3,)),
                        pltpu.VMEM((hp, nbp, HEAD_DIM), F32),
                        pltpu.VMEM((hp, nb, acc_rows, blk), BF16),
                        pltpu.VMEM((hp, nb, HEAD_DIM, blk), BF16),
                        pltpu.VMEM((hp, nb, nbp, blk), F32),
                        pltpu.VMEM((hp, MOBA_GROUP, blk, blk), F32),
                        pltpu.VMEM((hp, MOBA_GROUP, blk, blk), F32),
                        pltpu.VMEM((hp, MOBA_GROUP, SUBLANES, blk), F32),
                        pltpu.VMEM((hp, MOBA_GROUP, SUBLANES, blk), F32),
                        pltpu.VMEM((hp, MOBA_GROUP, blk, blk), BF16),
                        pltpu.VMEM((hp, MOBA_GROUP, blk, blk), BF16),
                        pltpu.VMEM((2, hp, acc_rows, blk), F32)],
        compiler_params=_params("parallel", "parallel", "arbitrary"),
        name="moba",
    )(z3)


def _hgrn_kernel(q_ref, f_ref, i_ref, og_ref, lb_ref, gain_ref, o_ref, st_ref, *, tt, chunk, sub, unroll):
    @pl.when(pl.program_id(2) == 0)
    def _():
        st_ref[...] = jnp.zeros_like(st_ref)

    n_blocks = chunk // sub
    stack = n_blocks * chunk
    lb = lb_ref[...]
    gain = gain_ref[...]
    tri = (lax.broadcasted_iota(jnp.int32, (chunk, chunk), 0)
           >= lax.broadcasted_iota(jnp.int32, (chunk, chunk), 1)).astype(BF16)
    r_i = lax.broadcasted_iota(jnp.int32, (chunk, stack), 0)
    c_i = lax.broadcasted_iota(jnp.int32, (chunk, stack), 1)
    keep = ((c_i // chunk) == (r_i // sub)) & ((c_i % chunk) <= r_i)
    row_c = lax.broadcasted_iota(jnp.int32, (chunk, HEAD_DIM), 0)

    def one_chunk(n, carry):
        off = pl.multiple_of(n * chunk, chunk)
        xq = q_ref[pl.ds(off, chunk), :].astype(F32)
        qs = xq * jax.nn.sigmoid(xq)
        f = lb + (1.0 - lb) * jax.nn.sigmoid(f_ref[pl.ds(off, chunk), :].astype(F32))
        logf = jnp.log(f)
        kk = 1.0 - f
        vb = i_ref[pl.ds(off, chunk), :]

        hi = logf.astype(BF16)
        r1 = logf - hi.astype(F32)
        mid = r1.astype(BF16)
        lo = (r1 - mid.astype(F32)).astype(BF16)
        b = _dot(tri, hi) + _dot(tri, mid) + _dot(tri, lo)
        b_last = b[chunk - 1:chunk, :]

        st = st_ref[...]
        qe = (qs * jnp.exp(b)).astype(BF16)
        o_inter = lax.dot_general(qe, st.astype(BF16), NT_DIMS, preferred_element_type=F32)

        ref_rows = jnp.concatenate(
            [jnp.broadcast_to(b[blk * sub:blk * sub + 1, :], (sub, HEAD_DIM)) for blk in range(n_blocks)],
            axis=0)
        qt = (qs * jnp.exp(b - ref_rows)).astype(BF16)
        kh = jnp.concatenate(
            [(kk * jnp.exp(jnp.where(row_c < (blk + 1) * sub, b[blk * sub:blk * sub + 1, :] - b, NEG_INF))
              ).astype(BF16) for blk in range(n_blocks)], axis=0)
        res = lax.dot_general(qt, kh, NT_DIMS, preferred_element_type=F32)
        a = jnp.where(keep, res, 0.0).astype(BF16)
        o = o_inter + _dot(a, jnp.concatenate([vb] * n_blocks, axis=0))

        kd = (kk * jnp.exp(b_last - b)).astype(BF16)
        st_ref[...] = st * jnp.exp(b_last) + lax.dot_general(vb, kd, TN_DIMS, preferred_element_type=F32)

        ms = jnp.mean(o * o, axis=-1, keepdims=True)
        on = o * lax.rsqrt(ms + RMS_EPS) * gain
        og = og_ref[pl.ds(off, chunk), :].astype(F32)
        o_ref[pl.ds(off, chunk), :] = (on * jax.nn.sigmoid(og)).astype(o_ref.dtype)
        return carry

    lax.fori_loop(0, tt // chunk, one_chunk, 0, unroll=unroll)


def _sigmoid(x):
    return 0.5 * jnp.tanh(0.5 * x) + 0.5


def _rows_to_blocks(rows, sub):
    return jnp.concatenate([jnp.broadcast_to(rows[k:k + 1, :], (sub, HEAD_DIM)) for k in range(rows.shape[0])],
                           axis=0)


def _hgrn_spans(q_ref, f_ref, i_ref, og_ref, lb_ref, gain_ref, o_ref, st_ref, *, n_spans, span, chunk, sub, hp,
                hooks=()):
    n_chunks = span // chunk
    nblk = chunk // sub
    n_all = span // sub
    stack = nblk * chunk
    r_s = lax.broadcasted_iota(jnp.int32, (span, span), 0)
    c_s = lax.broadcasted_iota(jnp.int32, (span, span), 1)
    tri = ((r_s >= c_s) & (r_s // chunk == c_s // chunk)).astype(BF16)
    r_i = lax.broadcasted_iota(jnp.int32, (chunk, stack), 0)
    c_i = lax.broadcasted_iota(jnp.int32, (chunk, stack), 1)
    keep = ((c_i // chunk) == (r_i // sub)) & ((c_i % chunk) <= r_i)
    heads = range(hp)
    cols = [slice(h * HEAD_DIM, (h + 1) * HEAD_DIM) for h in heads]
    f_mid = [0.5 + 0.5 * lb_ref[h] for h in heads]
    f_amp = [0.5 - 0.5 * lb_ref[h] for h in heads]

    def hook(k):
        if k < len(hooks):
            hooks[k]()

    def one_span(sidx, carry):
        rows = pl.ds(_aligned(sidx * span, span), span)

        qs, kk, vb, b = [], [], [], []
        for h in heads:
            xq = q_ref[rows, cols[h]].astype(F32)
            qs.append(xq * _sigmoid(xq))
            th = jnp.tanh(0.5 * f_ref[rows, cols[h]].astype(F32))
            logf = jnp.log(f_mid[h] + f_amp[h] * th)
            kk.append(f_amp[h] - f_amp[h] * th)
            vb.append(i_ref[rows, cols[h]])
            hi = logf.astype(BF16)
            r1 = logf - hi.astype(F32)
            mid = r1.astype(BF16)
            lo = (r1 - mid.astype(F32)).astype(BF16)
            b.append(_dot(tri, hi) + _dot(tri, mid) + _dot(tri, lo))

        hook(0)
        qe, b_last, res, upd = [], [], [], []
        for h in heads:
            first = jnp.concatenate([b[h][k * sub:k * sub + 1, :] for k in range(n_all)], axis=0)
            last = jnp.concatenate([b[h][k * sub + sub - 1:(k + 1) * sub, :] for k in range(n_all)], axis=0)
            qt = qs[h] * jnp.exp(b[h] - _rows_to_blocks(first, sub))
            kt = kk[h] * jnp.exp(_rows_to_blocks(last, sub) - b[h])
            bl = [last[(c + 1) * nblk - 1:(c + 1) * nblk, :] for c in range(n_chunks)]
            to_end = jnp.concatenate([bl[c] - last[c * nblk:(c + 1) * nblk, :] for c in range(n_chunks)], axis=0)
            qe.append((qt * _rows_to_blocks(jnp.exp(first), sub)).astype(BF16))
            kd = (kt * _rows_to_blocks(jnp.exp(to_end), sub)).astype(BF16)
            qtb = qt.astype(BF16)
            b_last.append(bl)
            res_h, upd_h = [], []
            for c in range(n_chunks):
                lo_r, hi_r = c * chunk, (c + 1) * chunk
                fr = first[c * nblk:(c + 1) * nblk, :]
                la = last[c * nblk:(c + 1) * nblk, :]
                g = jnp.exp(_rows_to_blocks(fr, nblk) - jnp.concatenate([la] * nblk, axis=0))
                ktc = kt[lo_r:hi_r, :]
                kh = jnp.concatenate(
                    [ktc[jb * sub:(jb + 1) * sub, :] * g[ib * nblk + jb:ib * nblk + jb + 1, :] if jb <= ib
                     else jnp.zeros((sub, HEAD_DIM), F32)
                     for ib in range(nblk) for jb in range(nblk)], axis=0).astype(BF16)
                res_h.append(lax.dot_general(qtb[lo_r:hi_r, :], kh, NT_DIMS, preferred_element_type=F32))
                upd_h.append(lax.dot_general(vb[h][lo_r:hi_r, :], kd[lo_r:hi_r, :], TN_DIMS,
                                             preferred_element_type=F32))
            res.append(res_h)
            upd.append(upd_h)

        hook(1)
        intra = [[_dot(jnp.where(keep, res[h][c], 0.0).astype(BF16),
                       jnp.concatenate([vb[h][c * chunk:(c + 1) * chunk, :]] * nblk, axis=0))
                  for c in range(n_chunks)] for h in heads]

        hook(2)
        st = [st_ref[h] for h in heads]
        for c in range(n_chunks):
            lo_r, hi_r = c * chunk, (c + 1) * chunk
            out_rows = pl.ds(_aligned(sidx * span + lo_r, chunk), chunk)
            for h in heads:
                o = intra[h][c] + lax.dot_general(qe[h][lo_r:hi_r, :], st[h].astype(BF16), NT_DIMS,
                                                  preferred_element_type=F32)
                st[h] = st[h] * jnp.exp(b_last[h][c]) + upd[h][c]
                ms = jnp.mean(o * o, axis=-1, keepdims=True)
                on = o * lax.rsqrt(ms + RMS_EPS) * gain_ref[h]
                og = og_ref[out_rows, cols[h]].astype(F32)
                o_ref[out_rows, cols[h]] = (on * _sigmoid(og)).astype(o_ref.dtype)
            hook(3 + c)
        for h in heads:
            st_ref[h] = st[h]
        return carry

    if n_spans == 1:
        one_span(0, 0)
    else:
        lax.fori_loop(0, n_spans, one_span, 0)


def _hgrn_safe(z, lb, gain, *, bsz, seq, heads, q_col, f_col, i_col, og_col):
    tt = _tile(seq, 256)
    nt = seq // tt
    kern = functools.partial(_hgrn_kernel, tt=tt, chunk=HGRN_SAFE_CHUNK, sub=1, unroll=False)

    def col_spec(col):
        return pl.BlockSpec((tt, HEAD_DIM), lambda b, h, t: (b * nt + t, col + h))

    head_spec = pl.BlockSpec((None, 1, HEAD_DIM), lambda b, h, t: (h, 0, 0))
    return pl.pallas_call(
        kern,
        grid=(bsz, heads, nt),
        in_specs=[col_spec(q_col), col_spec(f_col), col_spec(i_col), col_spec(og_col), head_spec, head_spec],
        out_specs=pl.BlockSpec((tt, HEAD_DIM), lambda b, h, t: (b * nt + t, h)),
        out_shape=jax.ShapeDtypeStruct((bsz * seq, heads * HEAD_DIM), BF16),
        scratch_shapes=[pltpu.VMEM((HEAD_DIM, HEAD_DIM), F32)],
        compiler_params=_params("parallel", "parallel", "arbitrary"),
        name="hgrn_safe",
    )(z, z, z, z, lb, gain)


def _hgrn_fast_ok(lb):
    return -(HGRN_SUB - 1) * jnp.log(jnp.min(lb)) <= HGRN_MAX_EXPONENT


def _mixout_kernel(*refs, alpha, n_gate, gw, nb, hgrn):
    if hgrn:
        (q_ref, f_ref, i_ref, og_ref, lb_ref, gain_ref), refs = refs[:6], refs[6:]
    (oa_lo_ref, oa_hi_ref), refs = refs[:2], refs[2:]
    if not hgrn:
        ob_in_ref, refs = refs[0], refs[1:]
    ga_refs, gb_refs, refs = refs[:n_gate], refs[n_gate:2 * n_gate], refs[2 * n_gate:]
    x_ref, wpa_ref, wpb_ref, wo_ref, g_ref, b_ref, o_ref = refs[:7]
    step = pl.program_id(0)
    if hgrn:
        st_ref, ob_buf_ref = refs[7:]
        tile = jnp.maximum(step - 1, 0)
        next_tile = jnp.minimum(step, pl.num_programs(0) - 2)
        slot = step % 2

        @pl.when(step == 0)
        def _():
            ob_buf_ref[...] = jnp.zeros_like(ob_buf_ref)

        @pl.when(next_tile % nb == 0)
        def _():
            st_ref[...] = jnp.zeros_like(st_ref)

        ob = ob_buf_ref[1 - slot]
    else:
        tile = step
        ob = ob_in_ref[...]
    in_lo = tile % nb < nb // 2
    oa_t = jnp.where(in_lo, oa_lo_ref[...], oa_hi_ref[...])
    merged, parts = {}, []

    def project(c):
        ya = lax.dot_general(oa_t, wpa_ref[:, c * gw:(c + 1) * gw], TN_DIMS, preferred_element_type=F32)
        yb = _dot(ob, wpb_ref[:, c * gw:(c + 1) * gw])
        merged[c] = (jax.nn.sigmoid(ga_refs[c][...].astype(F32)) * ya
                     + jax.nn.sigmoid(gb_refs[c][...].astype(F32)) * yb).astype(BF16)

    def output(c):
        parts.append(_dot(merged[c], wo_ref[c * gw:(c + 1) * gw, :]))

    def finish():
        o_ref[...] = _layer_norm(alpha * x_ref[...] + sum(parts[1:], parts[0]), g_ref[...], b_ref[...])

    def chain(*fs):
        return lambda: [f() for f in fs]

    steps = [functools.partial(project, 0)]
    steps += [chain(functools.partial(output, c - 1), functools.partial(project, c)) for c in range(1, n_gate)]
    pieces = steps + [functools.partial(output, n_gate - 1), finish]
    if hgrn:
        n_hooks = 3 + ob_buf_ref.shape[1] // hgrn["chunk"]
        _hgrn_spans(q_ref, f_ref, i_ref, og_ref, lb_ref, gain_ref, ob_buf_ref.at[slot], st_ref, n_spans=1,
                    span=ob_buf_ref.shape[1], hooks=pieces[:n_hooks], **hgrn)
        pieces = pieces[n_hooks:]
    for piece in pieces:
        piece()


def _mixout(o_a_halves, o_b, z, x, wpa, wpb, wo, ln_g, ln_b, *, alpha, ga_off, gb_off, nb, hgrn=None):
    m, d = x.shape
    gw = math.gcd(math.gcd(ga_off, gb_off), d)
    n_gate = d // gw
    tm = MOBA_BLOCK
    half = nb // 2
    n_tiles = m // tm
    if hgrn:
        hb = hgrn["heads"]
        b_width = hb * HEAD_DIM
        assert tm % HGRN_CHUNK == 0 and all(hgrn[k] % hb == 0 for k in ("q_col", "f_col", "i_col", "og_col"))
        mix_tile = lambda i: jnp.maximum(i - 1, 0)
        next_tile = lambda i: jnp.minimum(i, n_tiles - 1)
        settings = dict(chunk=HGRN_CHUNK, sub=HGRN_SUB, hp=hb)
    else:
        mix_tile = lambda i: i
        settings = None
    kern = functools.partial(_mixout_kernel, alpha=alpha, n_gate=n_gate, gw=gw, nb=nb, hgrn=settings)

    def gate_spec(off, c):
        return pl.BlockSpec((tm, gw), lambda i: (mix_tile(i), off // gw + c))

    def whole(a):
        return pl.BlockSpec(a.shape, lambda i: (0,) * a.ndim, pipeline_mode=pl.Buffered(1))

    def rows(a):
        return pl.BlockSpec((tm, a.shape[1]), lambda i: (mix_tile(i), 0))

    def attn_spec(pos):
        def index(i):
            t = mix_tile(i)
            return 0, (t // nb) * half + pos(t % nb)
        return pl.BlockSpec((o_a_halves[0].shape[0], tm), index)

    attn_specs = [attn_spec(lambda blk: jnp.minimum(blk, half - 1)),
                  attn_spec(lambda blk: nb - 1 - jnp.maximum(blk, half))]
    tail_specs = ([gate_spec(ga_off, c) for c in range(n_gate)] + [gate_spec(gb_off, c) for c in range(n_gate)]
                  + [rows(x), whole(wpa), whole(wpb), whole(wo), whole(ln_g), whole(ln_b)])
    tail_args = (*([z] * (2 * n_gate)), x, wpa, wpb, wo, ln_g, ln_b)
    if hgrn:
        def col_spec(col):
            return pl.BlockSpec((tm, b_width), lambda i: (next_tile(i), col // hb))
        in_specs = ([col_spec(hgrn[k]) for k in ("q_col", "f_col", "i_col", "og_col")]
                    + [whole(hgrn["lb"]), whole(hgrn["gain"])] + attn_specs + tail_specs)
        args = (z, z, z, z, hgrn["lb"], hgrn["gain"], *o_a_halves, *tail_args)
        scratch = [pltpu.VMEM((hb, HEAD_DIM, HEAD_DIM), F32),
                   pltpu.VMEM((2, tm, b_width), BF16)]
    else:
        in_specs = attn_specs + [rows(o_b)] + tail_specs
        args = (*o_a_halves, o_b, *tail_args)
        scratch = []
    return pl.pallas_call(
        kern,
        grid=(n_tiles + (1 if hgrn else 0),),
        in_specs=in_specs,
        out_specs=pl.BlockSpec((tm, d), lambda i: (mix_tile(i), 0)),
        out_shape=jax.ShapeDtypeStruct((m, d), F32),
        scratch_shapes=scratch,
        compiler_params=_params("arbitrary"),
        name="hgrn_mixout" if hgrn else "mixout",
    )(*args)


def _ffn_kernel(h_ref, wg_ref, wu_ref, wd_ref, g_ref, b_ref, o_ref, hb_ref, *, alpha, nf):
    f = pl.program_id(1)
    chunk = min(FFN_ROW_CHUNK, hb_ref.shape[0])

    def step(first, last):
        for r in range(0, hb_ref.shape[0], chunk):
            rows = slice(r, r + chunk)
            if first:
                hb = h_ref[rows, :].astype(BF16)
                hb_ref[rows, :] = hb
            else:
                hb = hb_ref[rows, :]
            gate = _dot(hb, wg_ref[...])
            up = _dot(hb, wu_ref[...])
            act = (gate * _sigmoid(gate) * up).astype(BF16)
            down = _dot(act, wd_ref[...])
            acc = down if first else o_ref[rows, :] + down
            o_ref[rows, :] = _layer_norm(alpha * h_ref[rows, :] + acc, g_ref[...], b_ref[...]) if last else acc

    if nf == 1:
        step(True, True)
    else:
        pl.when(f == 0)(functools.partial(step, True, False))
        pl.when(f == nf - 1)(functools.partial(step, False, True))
        if nf > 2:
            pl.when((f > 0) & (f < nf - 1))(functools.partial(step, False, False))


def _ffn(h, wg, wu, wd, ln_g, ln_b, *, alpha):
    m, d = h.shape
    hidden = wg.shape[1]
    tm, tf = _tile(m, 1024), _tile(hidden, 512)
    assert tm % min(FFN_ROW_CHUNK, tm) == 0
    nf = hidden // tf
    kern = functools.partial(_ffn_kernel, alpha=alpha, nf=nf)
    vec = pl.BlockSpec((1, d), lambda i, f: (0, 0))
    return pl.pallas_call(
        kern,
        grid=(m // tm, nf),
        in_specs=[pl.BlockSpec((tm, d), lambda i, f: (i, 0)),
                  pl.BlockSpec((d, tf), lambda i, f: (0, f)),
                  pl.BlockSpec((d, tf), lambda i, f: (0, f)),
                  pl.BlockSpec((tf, d), lambda i, f: (f, 0)),
                  vec, vec],
        out_specs=pl.BlockSpec((tm, d), lambda i, f: (i, 0)),
        out_shape=jax.ShapeDtypeStruct((m, d), F32),
        scratch_shapes=[pltpu.VMEM((tm, d), BF16)],
        compiler_params=_params("parallel", "arbitrary"),
        name="ffn",
    )(h, wg, wu, wd, ln_g, ln_b)


def kernel(x, w_in, w_proj_a, w_proj_b, w_out, hgrn_norm_g, hgrn_lb_logits, ln1_g, ln1_b,
           w_gate_ffn, w_up_ffn, w_down_ffn, ln2_g, ln2_b):
    bsz, seq, d = x.shape
    depth = w_in.shape[0]
    a_width = w_proj_a.shape[1]
    b_vwidth = w_proj_b.shape[1]
    b_fwidth = hgrn_lb_logits.shape[1]
    assert a_width % HEAD_DIM == 0 and b_vwidth == b_fwidth and b_fwidth % HEAD_DIM == 0
    assert w_in.shape[2] == 3 * a_width + 2 * b_fwidth + 2 * b_vwidth + 2 * d
    a_heads = a_width // HEAD_DIM
    b_heads = b_fwidth // HEAD_DIM
    qa_col, ka_col, va_col = 0, a_heads, 2 * a_heads
    qb_col = 3 * a_heads
    fb_col, ib_col, ogb_col = qb_col + b_heads, qb_col + 2 * b_heads, qb_col + 3 * b_heads
    ga_off = 3 * a_width + 2 * b_fwidth + 2 * b_vwidth
    gb_off = ga_off + d
    alpha = (2.0 * depth) ** 0.25

    lb_all = jnp.cumsum(jax.nn.softmax(hgrn_lb_logits.astype(F32), axis=0), axis=0)

    h = x.reshape(bsz * seq, d)
    for l in range(depth):
        z = _inproj(h, w_in[l].astype(BF16))
        o_a = _moba(z, bsz, seq, a_heads, qa_col, ka_col, va_col)
        lb = lb_all[l].reshape(b_heads, 1, HEAD_DIM)
        gain = hgrn_norm_g[l].astype(F32).reshape(b_heads, 1, HEAD_DIM)
        b_cols = dict(heads=b_heads, q_col=qb_col, f_col=fb_col, i_col=ib_col, og_col=ogb_col)
        mix = functools.partial(_mixout, alpha=alpha, ga_off=ga_off, gb_off=gb_off, nb=seq // MOBA_BLOCK)
        mix_weights = (w_proj_a[l].astype(BF16), w_proj_b[l].astype(BF16), w_out[l].astype(BF16),
                       ln1_g[l].reshape(1, d), ln1_b[l].reshape(1, d))

        def fused(o_a, z, h, lb, gain, *w):
            return mix(o_a, None, z, h, *w, hgrn=dict(lb=lb, gain=gain, **b_cols))

        def split(o_a, z, h, lb, gain, *w):
            o_b = _hgrn_safe(z, lb, gain, bsz=bsz, seq=seq, **b_cols)
            return mix(o_a, o_b, z, h, *w)

        h = lax.cond(_hgrn_fast_ok(lb), fused, split, o_a, z, h, lb, gain, *mix_weights)
        h = _ffn(h, w_gate_ffn[l].astype(BF16), w_up_ffn[l].astype(BF16), w_down_ffn[l].astype(BF16),
                 ln2_g[l].reshape(1, d), ln2_b[l].reshape(1, d), alpha=alpha)
    return h.reshape(bsz, seq, d)
```

```python
import functools
import math

import jax
import jax.numpy as jnp
from jax import lax
from jax.experimental import pallas as pl
from jax.experimental.pallas import tpu as pltpu

F32 = jnp.float32
BF16 = jnp.bfloat16

LANES = 128
SUBLANES = 8
HEAD_DIM = 128
MOBA_BLOCK = 256
MOBA_TOPK = 3
MOBA_GROUP = 2
MOBA_HEADS_PER_STEP = 2
MOBA_MAX_FLOOR = float(jnp.finfo(jnp.float32).min)
MOBA_ONES_ROWS = 16
LOG2_E = 1.4426950408889634
LN_EPS = 1e-5
RMS_EPS = 1e-6
NEG_INF = float("-inf")

HGRN_CHUNK = 64
HGRN_SUB = 16
HGRN_SAFE_CHUNK = 8
HGRN_MAX_EXPONENT = 60.0

VMEM_LIMIT_BYTES = 56 * 1024 * 1024
FFN_ROW_CHUNK = 512

NT_DIMS = (((1,), (1,)), ((), ()))
TN_DIMS = (((0,), (0,)), ((), ()))


def _tile(n, pref):
    if n <= pref:
        return n
    t = pref - pref % LANES
    while t >= LANES:
        if n % t == 0:
            return t
        t -= LANES
    raise ValueError(f"no lane-aligned tile for {n}")


def _params(*sem):
    return pltpu.CompilerParams(dimension_semantics=sem, vmem_limit_bytes=VMEM_LIMIT_BYTES)


def _aligned(offset, multiple):
    return offset if isinstance(offset, int) else pl.multiple_of(offset, multiple)


def _dot(a, b):
    return jnp.dot(a, b, preferred_element_type=F32)


def _layer_norm(pre, g, b):
    mu = jnp.mean(pre, axis=-1, keepdims=True)
    d = pre - mu
    var = jnp.mean(d * d, axis=-1, keepdims=True)
    return d * lax.rsqrt(var + LN_EPS) * g + b


def _inproj_kernel(x_ref, w_ref, o_ref, xb_ref):
    @pl.when(pl.program_id(1) == 0)
    def _():
        xb_ref[...] = x_ref[...].astype(BF16)

    o_ref[...] = _dot(xb_ref[...], w_ref[...]).astype(o_ref.dtype)


def _inproj(x, w):
    m, k = x.shape
    n = w.shape[1]
    tm, tn = _tile(m, 1024), _tile(n, 1024)
    return pl.pallas_call(
        _inproj_kernel,
        grid=(m // tm, n // tn),
        in_specs=[pl.BlockSpec((tm, k), lambda i, j: (i, 0)),
                  pl.BlockSpec((k, tn), lambda i, j: (0, j))],
        out_specs=pl.BlockSpec((tm, tn), lambda i, j: (i, j)),
        out_shape=jax.ShapeDtypeStruct((m, n), BF16),
        scratch_shapes=[pltpu.VMEM((tm, k), BF16)],
        compiler_params=_params("parallel", "arbitrary"),
        name="inproj",
    )(x, w)


def _moba_kernel(z_hbm, oa_ref, ob_ref, q_ref, k_ref, v_ref, dma_sem, kmean_ref, vt_ref, qts_ref, sel_ref,
                 sa_ref, sb_ref, ma_ref, mb_ref, pa_ref, pb_ref, acc_ref,
                 *, nb, nbp, log2_scale, hp, q_col, k_col, v_col):
    i = pl.program_id(2)
    blk = MOBA_BLOCK
    grp = MOBA_GROUP
    heads = range(hp)
    cols = [slice(h * HEAD_DIM, (h + 1) * HEAD_DIM) for h in heads]

    @pl.when(i == 0)
    def _():
        width = hp * HEAD_DIM
        copies = [
            pltpu.make_async_copy(
                z_hbm.at[pl.program_id(0), :, pl.ds((col // hp + pl.program_id(1)) * width, width)], dst,
                dma_sem.at[idx])
            for idx, (col, dst) in enumerate(((k_col, k_ref), (v_col, v_ref), (q_col, q_ref)))]
        for cp in copies:
            cp.start()
        copies[0].wait()
        kmean_ref[...] = jnp.zeros_like(kmean_ref)

        def key_means(n, carry):
            rows = pl.ds(pl.multiple_of(n * blk, blk), blk)
            for h in heads:
                kf = k_ref[rows, cols[h]].astype(F32)
                kmean_ref[h, pl.ds(n, 1), :] = jnp.sum(kf, axis=0, keepdims=True) * (1.0 / blk)
            return carry

        lax.fori_loop(0, nb, key_means, 0)
        copies[1].wait()

        def values(n, carry):
            rows = pl.ds(pl.multiple_of(n * blk, blk), blk)
            for h in heads:
                vt_ref[h, n, :HEAD_DIM, :] = v_ref[rows, cols[h]].astype(F32).T.astype(BF16)
                vt_ref[h, n, HEAD_DIM:, :] = jnp.ones((MOBA_ONES_ROWS, blk), BF16)
            return carry

        lax.fori_loop(0, nb, values, 0, unroll=2 if nb % 2 == 0 else 1)
        copies[2].wait()
        row = lax.broadcasted_iota(jnp.int32, (nbp, blk), 0)

        def select(n, carry):
            rows = pl.ds(pl.multiple_of(n * blk, blk), blk)
            for h in heads:
                qt32 = q_ref[rows, cols[h]].astype(F32).T
                qt = qt32.astype(BF16)
                qts_ref[h, n] = (qt32 * log2_scale).astype(BF16)
                km = kmean_ref[h]
                km_hi = km.astype(BF16)
                km_lo = (km - km_hi.astype(F32)).astype(BF16)
                gate = _dot(km_hi, qt) + _dot(km_lo, qt)
                g = jnp.where(row < n, gate, NEG_INF)
                sel = jnp.zeros((nbp, blk), dtype=jnp.bool_)
                for _ in range(MOBA_TOPK):
                    mx = jnp.max(g, axis=0, keepdims=True)
                    idx = jnp.min(jnp.where(g == mx, row, nbp), axis=0, keepdims=True)
                    pick = (row == idx) & (mx > NEG_INF)
                    sel = sel | pick
                    g = jnp.where(pick, NEG_INF, g)
                sel_ref[h, n] = sel.astype(F32)
            return carry

        lax.fori_loop(0, nb, select, 0, unroll=4 if nb % 4 == 0 else 1)

    blk_a, blk_b = i, nb - 1 - i
    n_a = (blk_a + grp) // grp
    n_total = nb // grp + 1

    def group(g):
        in_b = g >= n_a
        qb = jnp.where(in_b, blk_b, blk_a)
        n = jnp.where(in_b, n_total - 1 - g, g)
        ids = [qb - n * grp - c for c in range(grp)]
        return dict(qb=qb, slot=in_b.astype(jnp.int32), own=g in (0, n_total - 1),
                    restart=True if g == 0 else g == n_a,
                    js=[jnp.maximum(j, 0) for j in ids], oks=[(j >= 0).astype(F32) for j in ids])

    def scores_into(sbuf, mxbuf, d):
        for h in heads:
            for c, j in enumerate(d["js"]):
                s = _dot(k_ref[pl.ds(pl.multiple_of(j * blk, blk), blk), cols[h]], qts_ref[h, d["qb"]])
                if d["own"] and c == 0:
                    s = jnp.where(lax.broadcasted_iota(jnp.int32, (blk, blk), 0)
                                  <= lax.broadcasted_iota(jnp.int32, (blk, blk), 1), s, NEG_INF)
                sbuf[h, c] = s
                mxbuf[h, c] = jnp.broadcast_to(jnp.max(s, axis=0, keepdims=True), (SUBLANES, blk))

    def softmax_into(pbuf, sbuf, mxbuf, d, m_old):
        floor = jnp.full((1, blk), MOBA_MAX_FLOOR, F32)
        m_out, alpha_out = [], []
        for h in heads:
            chosen = [None if (d["own"] and c == 0)
                      else sel_ref[h, d["qb"], pl.ds(d["js"][c], 1), :] * d["oks"][c] > 0.0 for c in range(grp)]
            if d["restart"] is True:
                m_prev = floor
            else:
                m_prev = jnp.where(jnp.full((1, blk), d["restart"].astype(F32)) > 0.0, floor, m_old[h])
            m_new = m_prev
            for c in range(grp):
                cm = mxbuf[h, c, 0:1, :]
                m_new = jnp.maximum(m_new, cm if chosen[c] is None else jnp.where(chosen[c], cm, NEG_INF))
            for c in range(grp):
                mb = m_new if chosen[c] is None else jnp.where(chosen[c], m_new, jnp.inf)
                pbuf[h, c] = jnp.exp2(sbuf[h, c] - mb).astype(BF16)
            m_out.append(m_new)
            alpha_out.append(jnp.exp2(m_prev - m_new))
        return m_out, alpha_out

    def accumulate(pbuf, d, alpha):
        for h in heads:
            pv = None
            for c, j in enumerate(d["js"]):
                part = _dot(vt_ref[h, j], pbuf[h, c])
                pv = part if pv is None else pv + part
            acc_ref[d["slot"], h] = alpha[h] * acc_ref[d["slot"], h] + pv

    s_bufs, mx_bufs, p_bufs = (sa_ref, sb_ref), (ma_ref, mb_ref), (pa_ref, pb_ref)
    acc_ref[...] = jnp.zeros_like(acc_ref)
    groups = [group(g) for g in range(n_total)]
    m = [None for _ in heads]
    alpha = None
    scores_into(s_bufs[0], mx_bufs[0], groups[0])
    for g in range(n_total):
        if g + 1 < n_total:
            scores_into(s_bufs[(g + 1) % 2], mx_bufs[(g + 1) % 2], groups[g + 1])
        if g > 0:
            accumulate(p_bufs[(g - 1) % 2], groups[g - 1], alpha)
        m, alpha = softmax_into(p_bufs[g % 2], s_bufs[g % 2], mx_bufs[g % 2], groups[g], m)
    accumulate(p_bufs[(n_total - 1) % 2], groups[n_total - 1], alpha)
    for slot, out_ref in enumerate((oa_ref, ob_ref)):
        for h in heads:
            num = acc_ref[slot, h, :HEAD_DIM, :]
            den = acc_ref[slot, h, HEAD_DIM:HEAD_DIM + 1, :]
            out_ref[cols[h], :] = (num / den).astype(out_ref.dtype)


def _moba(z, bsz, seq, heads, q_col, k_col, v_col):
    blk = MOBA_BLOCK
    assert seq % (2 * blk) == 0 and MOBA_GROUP == 2
    nb = seq // blk
    half = nb // 2
    nbp = -(-nb // SUBLANES) * SUBLANES
    acc_rows = HEAD_DIM + MOBA_ONES_ROWS
    hp = math.gcd(heads, MOBA_HEADS_PER_STEP)
    assert all(col % hp == 0 for col in (q_col, k_col, v_col))
    width = hp * HEAD_DIM
    z3 = z.reshape(bsz, seq, z.shape[1])
    kern = functools.partial(_moba_kernel, nb=nb, nbp=nbp, log2_scale=HEAD_DIM ** -0.5 * LOG2_E, hp=hp,
                             q_col=q_col, k_col=k_col, v_col=v_col)
    seq_buf = pltpu.VMEM((seq, width), BF16)
    return pl.pallas_call(
        kern,
        grid=(bsz, heads // hp, half),
        in_specs=[pl.BlockSpec(memory_space=pl.ANY)],
        out_specs=[pl.BlockSpec((width, blk), lambda b, h, i: (h, b * half + i))] * 2,
        out_shape=[jax.ShapeDtypeStruct((heads * HEAD_DIM, bsz * seq // 2), BF16)] * 2,
        scratch_shapes=[seq_buf, seq_buf, seq_buf,
                        pltpu.SemaphoreType.DMA((3,)),
                        pltpu.VMEM((hp, nbp, HEAD_DIM), F32),
                        pltpu.VMEM((hp, nb, acc_rows, blk), BF16),
                        pltpu.VMEM((hp, nb, HEAD_DIM, blk), BF16),
                        pltpu.VMEM((hp, nb, nbp, blk), F32),
                        pltpu.VMEM((hp, MOBA_GROUP, blk, blk), F32),
                        pltpu.VMEM((hp, MOBA_GROUP, blk, blk), F32),
                        pltpu.VMEM((hp, MOBA_GROUP, SUBLANES, blk), F32),
                        pltpu.VMEM((hp, MOBA_GROUP, SUBLANES, blk), F32),
                        pltpu.VMEM((hp, MOBA_GROUP, blk, blk), BF16),
                        pltpu.VMEM((hp, MOBA_GROUP, blk, blk), BF16),
                        pltpu.VMEM((2, hp, acc_rows, blk), F32)],
        compiler_params=_params("parallel", "parallel", "arbitrary"),
        name="moba",
    )(z3)


def _hgrn_kernel(q_ref, f_ref, i_ref, og_ref, lb_ref, gain_ref, o_ref, st_ref, *, tt, chunk, sub, unroll):
    @pl.when(pl.program_id(2) == 0)
    def _():
        st_ref[...] = jnp.zeros_like(st_ref)

    n_blocks = chunk // sub
    stack = n_blocks * chunk
    lb = lb_ref[...]
    gain = gain_ref[...]
    tri = (lax.broadcasted_iota(jnp.int32, (chunk, chunk), 0)
           >= lax.broadcasted_iota(jnp.int32, (chunk, chunk), 1)).astype(BF16)
    r_i = lax.broadcasted_iota(jnp.int32, (chunk, stack), 0)
    c_i = lax.broadcasted_iota(jnp.int32, (chunk, stack), 1)
    keep = ((c_i // chunk) == (r_i // sub)) & ((c_i % chunk) <= r_i)
    row_c = lax.broadcasted_iota(jnp.int32, (chunk, HEAD_DIM), 0)

    def one_chunk(n, carry):
        off = pl.multiple_of(n * chunk, chunk)
        xq = q_ref[pl.ds(off, chunk), :].astype(F32)
        qs = xq * jax.nn.sigmoid(xq)
        f = lb + (1.0 - lb) * jax.nn.sigmoid(f_ref[pl.ds(off, chunk), :].astype(F32))
        logf = jnp.log(f)
        kk = 1.0 - f
        vb = i_ref[pl.ds(off, chunk), :]

        hi = logf.astype(BF16)
        r1 = logf - hi.astype(F32)
        mid = r1.astype(BF16)
        lo = (r1 - mid.astype(F32)).astype(BF16)
        b = _dot(tri, hi) + _dot(tri, mid) + _dot(tri, lo)
        b_last = b[chunk - 1:chunk, :]

        st = st_ref[...]
        qe = (qs * jnp.exp(b)).astype(BF16)
        o_inter = lax.dot_general(qe, st.astype(BF16), NT_DIMS, preferred_element_type=F32)

        ref_rows = jnp.concatenate(
            [jnp.broadcast_to(b[blk * sub:blk * sub + 1, :], (sub, HEAD_DIM)) for blk in range(n_blocks)],
            axis=0)
        qt = (qs * jnp.exp(b - ref_rows)).astype(BF16)
        kh = jnp.concatenate(
            [(kk * jnp.exp(jnp.where(row_c < (blk + 1) * sub, b[blk * sub:blk * sub + 1, :] - b, NEG_INF))
              ).astype(BF16) for blk in range(n_blocks)], axis=0)
        res = lax.dot_general(qt, kh, NT_DIMS, preferred_element_type=F32)
        a = jnp.where(keep, res, 0.0).astype(BF16)
        o = o_inter + _dot(a, jnp.concatenate([vb] * n_blocks, axis=0))

        kd = (kk * jnp.exp(b_last - b)).astype(BF16)
        st_ref[...] = st * jnp.exp(b_last) + lax.dot_general(vb, kd, TN_DIMS, preferred_element_type=F32)

        ms = jnp.mean(o * o, axis=-1, keepdims=True)
        on = o * lax.rsqrt(ms + RMS_EPS) * gain
        og = og_ref[pl.ds(off, chunk), :].astype(F32)
        o_ref[pl.ds(off, chunk), :] = (on * jax.nn.sigmoid(og)).astype(o_ref.dtype)
        return carry

    lax.fori_loop(0, tt // chunk, one_chunk, 0, unroll=unroll)


def _sigmoid(x):
    return 0.5 * jnp.tanh(0.5 * x) + 0.5


def _rows_to_blocks(rows, sub):
    return jnp.concatenate([jnp.broadcast_to(rows[k:k + 1, :], (sub, HEAD_DIM)) for k in range(rows.shape[0])],
                           axis=0)


def _hgrn_spans(q_ref, f_ref, i_ref, og_ref, lb_ref, gain_ref, o_ref, st_ref, *, n_spans, span, chunk, sub, hp,
                hooks=()):
    n_chunks = span // chunk
    nblk = chunk // sub
    n_all = span // sub
    stack = nblk * chunk
    r_s = lax.broadcasted_iota(jnp.int32, (span, span), 0)
    c_s = lax.broadcasted_iota(jnp.int32, (span, span), 1)
    tri = ((r_s >= c_s) & (r_s // chunk == c_s // chunk)).astype(BF16)
    r_i = lax.broadcasted_iota(jnp.int32, (chunk, stack), 0)
    c_i = lax.broadcasted_iota(jnp.int32, (chunk, stack), 1)
    keep = ((c_i // chunk) == (r_i // sub)) & ((c_i % chunk) <= r_i)
    heads = range(hp)
    cols = [slice(h * HEAD_DIM, (h + 1) * HEAD_DIM) for h in heads]
    f_mid = [0.5 + 0.5 * lb_ref[h] for h in heads]
    f_amp = [0.5 - 0.5 * lb_ref[h] for h in heads]

    def hook(k):
        if k < len(hooks):
            hooks[k]()

    def one_span(sidx, carry):
        rows = pl.ds(_aligned(sidx * span, span), span)

        qs, kk, vb, b = [], [], [], []
        for h in heads:
            xq = q_ref[rows, cols[h]].astype(F32)
            qs.append(xq * _sigmoid(xq))
            th = jnp.tanh(0.5 * f_ref[rows, cols[h]].astype(F32))
            logf = jnp.log(f_mid[h] + f_amp[h] * th)
            kk.append(f_amp[h] - f_amp[h] * th)
            vb.append(i_ref[rows, cols[h]])
            hi = logf.astype(BF16)
            r1 = logf - hi.astype(F32)
            mid = r1.astype(BF16)
            lo = (r1 - mid.astype(F32)).astype(BF16)
            b.append(_dot(tri, hi) + _dot(tri, mid) + _dot(tri, lo))

        hook(0)
        qe, b_last, res, upd = [], [], [], []
        for h in heads:
            first = jnp.concatenate([b[h][k * sub:k * sub + 1, :] for k in range(n_all)], axis=0)
            last = jnp.concatenate([b[h][k * sub + sub - 1:(k + 1) * sub, :] for k in range(n_all)], axis=0)
            qt = qs[h] * jnp.exp(b[h] - _rows_to_blocks(first, sub))
            kt = kk[h] * jnp.exp(_rows_to_blocks(last, sub) - b[h])
            bl = [last[(c + 1) * nblk - 1:(c + 1) * nblk, :] for c in range(n_chunks)]
            to_end = jnp.concatenate([bl[c] - last[c * nblk:(c + 1) * nblk, :] for c in range(n_chunks)], axis=0)
            qe.append((qt * _rows_to_blocks(jnp.exp(first), sub)).astype(BF16))
            kd = (kt * _rows_to_blocks(jnp.exp(to_end), sub)).astype(BF16)
            qtb = qt.astype(BF16)
            b_last.append(bl)
            res_h, upd_h = [], []
            for c in range(n_chunks):
                lo_r, hi_r = c * chunk, (c + 1) * chunk
                fr = first[c * nblk:(c + 1) * nblk, :]
                la = last[c * nblk:(c + 1) * nblk, :]
                g = jnp.exp(_rows_to_blocks(fr, nblk) - jnp.concatenate([la] * nblk, axis=0))
                ktc = kt[lo_r:hi_r, :]
                kh = jnp.concatenate(
                    [ktc[jb * sub:(jb + 1) * sub, :] * g[ib * nblk + jb:ib * nblk + jb + 1, :] if jb <= ib
                     else jnp.zeros((sub, HEAD_DIM), F32)
                     for ib in range(nblk) for jb in range(nblk)], axis=0).astype(BF16)
                res_h.append(lax.dot_general(qtb[lo_r:hi_r, :], kh, NT_DIMS, preferred_element_type=F32))
                upd_h.append(lax.dot_general(vb[h][lo_r:hi_r, :], kd[lo_r:hi_r, :], TN_DIMS,
                                             preferred_element_type=F32))
            res.append(res_h)
            upd.append(upd_h)

        hook(1)
        intra = [[_dot(jnp.where(keep, res[h][c], 0.0).astype(BF16),
                       jnp.concatenate([vb[h][c * chunk:(c + 1) * chunk, :]] * nblk, axis=0))
                  for c in range(n_chunks)] for h in heads]

        hook(2)
        st = [st_ref[h] for h in heads]
        for c in range(n_chunks):
            lo_r, hi_r = c * chunk, (c + 1) * chunk
            out_rows = pl.ds(_aligned(sidx * span + lo_r, chunk), chunk)
            for h in heads:
                o = intra[h][c] + lax.dot_general(qe[h][lo_r:hi_r, :], st[h].astype(BF16), NT_DIMS,
                                                  preferred_element_type=F32)
                st[h] = st[h] * jnp.exp(b_last[h][c]) + upd[h][c]
                ms = jnp.mean(o * o, axis=-1, keepdims=True)
                on = o * lax.rsqrt(ms + RMS_EPS) * gain_ref[h]
                og = og_ref[out_rows, cols[h]].astype(F32)
                o_ref[out_rows, cols[h]] = (on * _sigmoid(og)).astype(o_ref.dtype)
            hook(3 + c)
        for h in heads:
            st_ref[h] = st[h]
        return carry

    if n_spans == 1:
        one_span(0, 0)
    else:
        lax.fori_loop(0, n_spans, one_span, 0)


def _hgrn_safe(z, lb, gain, *, bsz, seq, heads, q_col, f_col, i_col, og_col):
    tt = _tile(seq, 256)
    nt = seq // tt
    kern = functools.partial(_hgrn_kernel, tt=tt, chunk=HGRN_SAFE_CHUNK, sub=1, unroll=False)

    def col_spec(col):
        return pl.BlockSpec((tt, HEAD_DIM), lambda b, h, t: (b * nt + t, col + h))

    head_spec = pl.BlockSpec((None, 1, HEAD_DIM), lambda b, h, t: (h, 0, 0))
    return pl.pallas_call(
        kern,
        grid=(bsz, heads, nt),
        in_specs=[col_spec(q_col), col_spec(f_col), col_spec(i_col), col_spec(og_col), head_spec, head_spec],
        out_specs=pl.BlockSpec((tt, HEAD_DIM), lambda b, h, t: (b * nt + t, h)),
        out_shape=jax.ShapeDtypeStruct((bsz * seq, heads * HEAD_DIM), BF16),
        scratch_shapes=[pltpu.VMEM((HEAD_DIM, HEAD_DIM), F32)],
        compiler_params=_params("parallel", "parallel", "arbitrary"),
        name="hgrn_safe",
    )(z, z, z, z, lb, gain)


def _hgrn_fast_ok(lb):
    return -(HGRN_SUB - 1) * jnp.log(jnp.min(lb)) <= HGRN_MAX_EXPONENT


def _mixout_kernel(*refs, alpha, n_gate, gw, nb, hgrn):
    if hgrn:
        (q_ref, f_ref, i_ref, og_ref, lb_ref, gain_ref), refs = refs[:6], refs[6:]
    (oa_lo_ref, oa_hi_ref), refs = refs[:2], refs[2:]
    if not hgrn:
        ob_in_ref, refs = refs[0], refs[1:]
    ga_refs, gb_refs, refs = refs[:n_gate], refs[n_gate:2 * n_gate], refs[2 * n_gate:]
    x_ref, wpa_ref, wpb_ref, wo_ref, g_ref, b_ref, o_ref = refs[:7]
    step = pl.program_id(0)
    if hgrn:
        st_ref, ob_buf_ref = refs[7:]
        tile = jnp.maximum(step - 1, 0)
        next_tile = jnp.minimum(step, pl.num_programs(0) - 2)
        slot = step % 2

        @pl.when(step == 0)
        def _():
            ob_buf_ref[...] = jnp.zeros_like(ob_buf_ref)

        @pl.when(next_tile % nb == 0)
        def _():
            st_ref[...] = jnp.zeros_like(st_ref)

        ob = ob_buf_ref[1 - slot]
    else:
        tile = step
        ob = ob_in_ref[...]
    in_lo = tile % nb < nb // 2
    oa_t = jnp.where(in_lo, oa_lo_ref[...], oa_hi_ref[...])
    merged, parts = {}, []

    def project(c):
        ya = lax.dot_general(oa_t, wpa_ref[:, c * gw:(c + 1) * gw], TN_DIMS, preferred_element_type=F32)
        yb = _dot(ob, wpb_ref[:, c * gw:(c + 1) * gw])
        merged[c] = (jax.nn.sigmoid(ga_refs[c][...].astype(F32)) * ya
                     + jax.nn.sigmoid(gb_refs[c][...].astype(F32)) * yb).astype(BF16)

    def output(c):
        parts.append(_dot(merged[c], wo_ref[c * gw:(c + 1) * gw, :]))

    def finish():
        o_ref[...] = _layer_norm(alpha * x_ref[...] + sum(parts[1:], parts[0]), g_ref[...], b_ref[...])

    def chain(*fs):
        return lambda: [f() for f in fs]

    steps = [functools.partial(project, 0)]
    steps += [chain(functools.partial(output, c - 1), functools.partial(project, c)) for c in range(1, n_gate)]
    pieces = steps + [functools.partial(output, n_gate - 1), finish]
    if hgrn:
        n_hooks = 3 + ob_buf_ref.shape[1] // hgrn["chunk"]
        _hgrn_spans(q_ref, f_ref, i_ref, og_ref, lb_ref, gain_ref, ob_buf_ref.at[slot], st_ref, n_spans=1,
                    span=ob_buf_ref.shape[1], hooks=pieces[:n_hooks], **hgrn)
        pieces = pieces[n_hooks:]
    for piece in pieces:
        piece()


def _mixout(o_a_halves, o_b, z, x, wpa, wpb, wo, ln_g, ln_b, *, alpha, ga_off, gb_off, nb, hgrn=None):
    m, d = x.shape
    gw = math.gcd(math.gcd(ga_off, gb_off), d)
    n_gate = d // gw
    tm = MOBA_BLOCK
    half = nb // 2
    n_tiles = m // tm
    if hgrn:
        hb = hgrn["heads"]
        b_width = hb * HEAD_DIM
        assert tm % HGRN_CHUNK == 0 and all(hgrn[k] % hb == 0 for k in ("q_col", "f_col", "i_col", "og_col"))
        mix_tile = lambda i: jnp.maximum(i - 1, 0)
        next_tile = lambda i: jnp.minimum(i, n_tiles - 1)
        settings = dict(chunk=HGRN_CHUNK, sub=HGRN_SUB, hp=hb)
    else:
        mix_tile = lambda i: i
        settings = None
    kern = functools.partial(_mixout_kernel, alpha=alpha, n_gate=n_gate, gw=gw, nb=nb, hgrn=settings)

    def gate_spec(off, c):
        return pl.BlockSpec((tm, gw), lambda i: (mix_tile(i), off // gw + c))

    def whole(a):
        return pl.BlockSpec(a.shape, lambda i: (0,) * a.ndim, pipeline_mode=pl.Buffered(1))

    def rows(a):
        return pl.BlockSpec((tm, a.shape[1]), lambda i: (mix_tile(i), 0))

    def attn_spec(pos):
        def index(i):
            t = mix_tile(i)
            return 0, (t // nb) * half + pos(t % nb)
        return pl.BlockSpec((o_a_halves[0].shape[0], tm), index)

    attn_specs = [attn_spec(lambda blk: jnp.minimum(blk, half - 1)),
                  attn_spec(lambda blk: nb - 1 - jnp.maximum(blk, half))]
    tail_specs = ([gate_spec(ga_off, c) for c in range(n_gate)] + [gate_spec(gb_off, c) for c in range(n_gate)]
                  + [rows(x), whole(wpa), whole(wpb), whole(wo), whole(ln_g), whole(ln_b)])
    tail_args = (*([z] * (2 * n_gate)), x, wpa, wpb, wo, ln_g, ln_b)
    if hgrn:
        def col_spec(col):
            return pl.BlockSpec((tm, b_width), lambda i: (next_tile(i), col // hb))
        in_specs = ([col_spec(hgrn[k]) for k in ("q_col", "f_col", "i_col", "og_col")]
                    + [whole(hgrn["lb"]), whole(hgrn["gain"])] + attn_specs + tail_specs)
        args = (z, z, z, z, hgrn["lb"], hgrn["gain"], *o_a_halves, *tail_args)
        scratch = [pltpu.VMEM((hb, HEAD_DIM, HEAD_DIM), F32),
                   pltpu.VMEM((2, tm, b_width), BF16)]
    else:
        in_specs = attn_specs + [rows(o_b)] + tail_specs
        args = (*o_a_halves, o_b, *tail_args)
        scratch = []
    return pl.pallas_call(
        kern,
        grid=(n_tiles + (1 if hgrn else 0),),
        in_specs=in_specs,
        out_specs=pl.BlockSpec((tm, d), lambda i: (mix_tile(i), 0)),
        out_shape=jax.ShapeDtypeStruct((m, d), F32),
        scratch_shapes=scratch,
        compiler_params=_params("arbitrary"),
        name="hgrn_mixout" if hgrn else "mixout",
    )(*args)


def _ffn_kernel(h_ref, wg_ref, wu_ref, wd_ref, g_ref, b_ref, o_ref, hb_ref, *, alpha, nf):
    f = pl.program_id(1)
    chunk = min(FFN_ROW_CHUNK, hb_ref.shape[0])

    def step(first, last):
        for r in range(0, hb_ref.shape[0], chunk):
            rows = slice(r, r + chunk)
            if first:
                hb = h_ref[rows, :].astype(BF16)
                hb_ref[rows, :] = hb
            else:
                hb = hb_ref[rows, :]
            gate = _dot(hb, wg_ref[...])
            up = _dot(hb, wu_ref[...])
            act = (gate * _sigmoid(gate) * up).astype(BF16)
            down = _dot(act, wd_ref[...])
            acc = down if first else o_ref[rows, :] + down
            o_ref[rows, :] = _layer_norm(alpha * h_ref[rows, :] + acc, g_ref[...], b_ref[...]) if last else acc

    if nf == 1:
        step(True, True)
    else:
        pl.when(f == 0)(functools.partial(step, True, False))
        pl.when(f == nf - 1)(functools.partial(step, False, True))
        if nf > 2:
            pl.when((f > 0) & (f < nf - 1))(functools.partial(step, False, False))


def _ffn(h, wg, wu, wd, ln_g, ln_b, *, alpha):
    m, d = h.shape
    hidden = wg.shape[1]
    tm, tf = _tile(m, 1024), _tile(hidden, 512)
    assert tm % min(FFN_ROW_CHUNK, tm) == 0
    nf = hidden // tf
    kern = functools.partial(_ffn_kernel, alpha=alpha, nf=nf)
    vec = pl.BlockSpec((1, d), lambda i, f: (0, 0))
    return pl.pallas_call(
        kern,
        grid=(m // tm, nf),
        in_specs=[pl.BlockSpec((tm, d), lambda i, f: (i, 0)),
                  pl.BlockSpec((d, tf), lambda i, f: (0, f)),
                  pl.BlockSpec((d, tf), lambda i, f: (0, f)),
                  pl.BlockSpec((tf, d), lambda i, f: (f, 0)),
                  vec, vec],
        out_specs=pl.BlockSpec((tm, d), lambda i, f: (i, 0)),
        out_shape=jax.ShapeDtypeStruct((m, d), F32),
        scratch_shapes=[pltpu.VMEM((tm, d), BF16)],
        compiler_params=_params("parallel", "arbitrary"),
        name="ffn",
    )(h, wg, wu, wd, ln_g, ln_b)


def kernel(x, w_in, w_proj_a, w_proj_b, w_out, hgrn_norm_g, hgrn_lb_logits, ln1_g, ln1_b,
           w_gate_ffn, w_up_ffn, w_down_ffn, ln2_g, ln2_b):
    bsz, seq, d = x.shape
    depth = w_in.shape[0]
    a_width = w_proj_a.shape[1]
    b_vwidth = w_proj_b.shape[1]
    b_fwidth = hgrn_lb_logits.shape[1]
    assert a_width % HEAD_DIM == 0 and b_vwidth == b_fwidth and b_fwidth % HEAD_DIM == 0
    assert w_in.shape[2] == 3 * a_width + 2 * b_fwidth + 2 * b_vwidth + 2 * d
    a_heads = a_width // HEAD_DIM
    b_heads = b_fwidth // HEAD_DIM
    qa_col, ka_col, va_col = 0, a_heads, 2 * a_heads
    qb_col = 3 * a_heads
    fb_col, ib_col, ogb_col = qb_col + b_heads, qb_col + 2 * b_heads, qb_col + 3 * b_heads
    ga_off = 3 * a_width + 2 * b_fwidth + 2 * b_vwidth
    gb_off = ga_off + d
    alpha = (2.0 * depth) ** 0.25

    lb_all = jnp.cumsum(jax.nn.softmax(hgrn_lb_logits.astype(F32), axis=0), axis=0)

    h = x.reshape(bsz * seq, d)
    for l in range(depth):
        z = _inproj(h, w_in[l].astype(BF16))
        o_a = _moba(z, bsz, seq, a_heads, qa_col, ka_col, va_col)
        lb = lb_all[l].reshape(b_heads, 1, HEAD_DIM)
        gain = hgrn_norm_g[l].astype(F32).reshape(b_heads, 1, HEAD_DIM)
        b_cols = dict(heads=b_heads, q_col=qb_col, f_col=fb_col, i_col=ib_col, og_col=ogb_col)
        mix = functools.partial(_mixout, alpha=alpha, ga_off=ga_off, gb_off=gb_off, nb=seq // MOBA_BLOCK)
        mix_weights = (w_proj_a[l].astype(BF16), w_proj_b[l].astype(BF16), w_out[l].astype(BF16),
                       ln1_g[l].reshape(1, d), ln1_b[l].reshape(1, d))

        def fused(o_a, z, h, lb, gain, *w):
            return mix(o_a, None, z, h, *w, hgrn=dict(lb=lb, gain=gain, **b_cols))

        def split(o_a, z, h, lb, gain, *w):
            o_b = _hgrn_safe(z, lb, gain, bsz=bsz, seq=seq, **b_cols)
            return mix(o_a, o_b, z, h, *w)

        h = lax.cond(_hgrn_fast_ok(lb), fused, split, o_a, z, h, lb, gain, *mix_weights)
        h = _ffn(h, w_gate_ffn[l].astype(BF16), w_up_ffn[l].astype(BF16), w_down_ffn[l].astype(BF16),
                 ln2_g[l].reshape(1, d), ln2_b[l].reshape(1, d), alpha=alpha)
    return h.reshape(bsz, seq, d)
```

```python
import functools
import math

import jax
import jax.numpy as jnp
from jax import lax
from jax.experimental import pallas as pl
from jax.experimental.pallas import tpu as pltpu

F32 = jnp.float32
BF16 = jnp.bfloat16

LANES = 128
SUBLANES = 8
BF16_SUBLANES = 16
CAST_ROW_WIDTH = 16 * LANES
HEAD_DIM = 128
MOBA_BLOCK = 256
MOBA_TOPK = 3
MOBA_GROUP = 2
MOBA_HEADS_PER_STEP = 2
MOBA_MAX_FLOOR = float(jnp.finfo(jnp.float32).min)
MOBA_ONES_ROWS = 16
LOG2_E = 1.4426950408889634
LN_EPS = 1e-5
RMS_EPS = 1e-6
NEG_INF = float("-inf")

HGRN_CHUNK = 64
HGRN_SUB = 16
HGRN_SAFE_CHUNK = 8
HGRN_MAX_EXPONENT = 60.0

VMEM_LIMIT_BYTES = 56 * 1024 * 1024
FFN_ROW_CHUNK = 512

NT_DIMS = (((1,), (1,)), ((), ()))
TN_DIMS = (((0,), (0,)), ((), ()))


def _tile(n, pref):
    if n <= pref:
        return n
    t = pref - pref % LANES
    while t >= LANES:
        if n % t == 0:
            return t
        t -= LANES
    raise ValueError(f"no lane-aligned tile for {n}")


def _params(*sem):
    return pltpu.CompilerParams(dimension_semantics=sem, vmem_limit_bytes=VMEM_LIMIT_BYTES)


def _aligned(offset, multiple):
    return offset if isinstance(offset, int) else pl.multiple_of(offset, multiple)


def _dot(a, b):
    return jnp.dot(a, b, preferred_element_type=F32)


def _layer_norm(pre, g, b):
    mu = jnp.mean(pre, axis=-1, keepdims=True)
    d = pre - mu
    var = jnp.mean(d * d, axis=-1, keepdims=True)
    return d * lax.rsqrt(var + LN_EPS) * g + b


def _inproj_kernel(*refs, n_cast):
    x_ref, w_ref = refs[:2]
    cast_in, refs = refs[2:2 + n_cast], refs[2 + n_cast:]
    o_ref, cast_out, xb_ref = refs[0], refs[1:1 + n_cast], refs[1 + n_cast]

    @pl.when(pl.program_id(1) == 0)
    def _():
        xb_ref[...] = x_ref[...].astype(BF16)

    o_ref[...] = _dot(xb_ref[...], w_ref[...]).astype(o_ref.dtype)
    for src, dst in zip(cast_in, cast_out):
        dst[...] = src[...].astype(BF16)


def _inproj(x, w, cast=()):
    m, k = x.shape
    n = w.shape[1]
    tm, tn = _tile(m, 1024), _tile(n, 1024)
    steps = (m // tm) * (n // tn)
    rows = [wt.size // CAST_ROW_WIDTH for wt in cast]
    ride = bool(cast) and all(wt.size % CAST_ROW_WIDTH == 0 and r % (steps * BF16_SUBLANES) == 0
                              for wt, r in zip(cast, rows))
    if not ride:
        rows = []
    flat = [wt.reshape(r, CAST_ROW_WIDTH) for wt, r in zip(cast, rows)]
    slice_specs = [pl.BlockSpec((r // steps, CAST_ROW_WIDTH), lambda i, j: (i * (n // tn) + j, 0)) for r in rows]
    outs = pl.pallas_call(
        functools.partial(_inproj_kernel, n_cast=len(flat)),
        grid=(m // tm, n // tn),
        in_specs=[pl.BlockSpec((tm, k), lambda i, j: (i, 0)),
                  pl.BlockSpec((k, tn), lambda i, j: (0, j))] + slice_specs,
        out_specs=[pl.BlockSpec((tm, tn), lambda i, j: (i, j))] + slice_specs,
        out_shape=[jax.ShapeDtypeStruct((m, n), BF16)] + [jax.ShapeDtypeStruct(f.shape, BF16) for f in flat],
        scratch_shapes=[pltpu.VMEM((tm, k), BF16)],
        compiler_params=_params("parallel", "arbitrary"),
        name="inproj",
    )(x, w, *flat)
    if ride:
        return outs[0], [o.reshape(wt.shape) for o, wt in zip(outs[1:], cast)]
    return outs[0], [wt.astype(BF16) for wt in cast]


def _moba_kernel(z_hbm, oa_ref, ob_ref, q_ref, k_ref, v_ref, dma_sem, kmean_ref, vt_ref, qts_ref, sel_ref,
                 sa_ref, sb_ref, ma_ref, mb_ref, pa_ref, pb_ref, acc_ref,
                 *, nb, nbp, log2_scale, hp, q_col, k_col, v_col):
    i = pl.program_id(2)
    blk = MOBA_BLOCK
    grp = MOBA_GROUP
    heads = range(hp)
    cols = [slice(h * HEAD_DIM, (h + 1) * HEAD_DIM) for h in heads]

    @pl.when(i == 0)
    def _():
        width = hp * HEAD_DIM
        copies = [
            pltpu.make_async_copy(
                z_hbm.at[pl.program_id(0), :, pl.ds((col // hp + pl.program_id(1)) * width, width)], dst,
                dma_sem.at[idx])
            for idx, (col, dst) in enumerate(((k_col, k_ref), (v_col, v_ref), (q_col, q_ref)))]
        for cp in copies:
            cp.start()
        copies[0].wait()
        kmean_ref[...] = jnp.zeros_like(kmean_ref)

        def key_means(n, carry):
            rows = pl.ds(pl.multiple_of(n * blk, blk), blk)
            for h in heads:
                kf = k_ref[rows, cols[h]].astype(F32)
                kmean_ref[h, pl.ds(n, 1), :] = jnp.sum(kf, axis=0, keepdims=True) * (1.0 / blk)
            return carry

        lax.fori_loop(0, nb, key_means, 0)
        copies[1].wait()

        def values(n, carry):
            rows = pl.ds(pl.multiple_of(n * blk, blk), blk)
            for h in heads:
                vt_ref[h, n, :HEAD_DIM, :] = v_ref[rows, cols[h]].astype(F32).T.astype(BF16)
                vt_ref[h, n, HEAD_DIM:, :] = jnp.ones((MOBA_ONES_ROWS, blk), BF16)
            return carry

        lax.fori_loop(0, nb, values, 0, unroll=2 if nb % 2 == 0 else 1)
        copies[2].wait()
        row = lax.broadcasted_iota(jnp.int32, (nbp, blk), 0)

        def select(n, carry):
            rows = pl.ds(pl.multiple_of(n * blk, blk), blk)
            for h in heads:
                qt32 = q_ref[rows, cols[h]].astype(F32).T
                qt = qt32.astype(BF16)
                qts_ref[h, n] = (qt32 * log2_scale).astype(BF16)
                km = kmean_ref[h]
                km_hi = km.astype(BF16)
                km_lo = (km - km_hi.astype(F32)).astype(BF16)
                gate = _dot(km_hi, qt) + _dot(km_lo, qt)
                g = jnp.where(row < n, gate, NEG_INF)
                sel = jnp.zeros((nbp, blk), dtype=jnp.bool_)
                for _ in range(MOBA_TOPK):
                    mx = jnp.max(g, axis=0, keepdims=True)
                    idx = jnp.min(jnp.where(g == mx, row, nbp), axis=0, keepdims=True)
                    pick = (row == idx) & (mx > NEG_INF)
                    sel = sel | pick
                    g = jnp.where(pick, NEG_INF, g)
                sel_ref[h, n] = sel.astype(F32)
            return carry

        lax.fori_loop(0, nb, select, 0, unroll=4 if nb % 4 == 0 else 1)

    blk_a, blk_b = i, nb - 1 - i
    n_a = (blk_a + grp) // grp
    n_total = nb // grp + 1

    def group(g):
        in_b = g >= n_a
        qb = jnp.where(in_b, blk_b, blk_a)
        n = jnp.where(in_b, n_total - 1 - g, g)
        ids = [qb - n * grp - c for c in range(grp)]
        return dict(qb=qb, slot=in_b.astype(jnp.int32), own=g in (0, n_total - 1),
                    restart=True if g == 0 else g == n_a,
                    js=[jnp.maximum(j, 0) for j in ids], oks=[(j >= 0).astype(F32) for j in ids])

    def scores_into(sbuf, mxbuf, d):
        for h in heads:
            for c, j in enumerate(d["js"]):
                s = _dot(k_ref[pl.ds(pl.multiple_of(j * blk, blk), blk), cols[h]], qts_ref[h, d["qb"]])
                if d["own"] and c == 0:
                    s = jnp.where(lax.broadcasted_iota(jnp.int32, (blk, blk), 0)
                                  <= lax.broadcasted_iota(jnp.int32, (blk, blk), 1), s, NEG_INF)
                sbuf[h, c] = s
                mxbuf[h, c] = jnp.broadcast_to(jnp.max(s, axis=0, keepdims=True), (SUBLANES, blk))

    def softmax_into(pbuf, sbuf, mxbuf, d, m_old):
        floor = jnp.full((1, blk), MOBA_MAX_FLOOR, F32)
        m_out, alpha_out = [], []
        for h in heads:
            chosen = [None if (d["own"] and c == 0)
                      else sel_ref[h, d["qb"], pl.ds(d["js"][c], 1), :] * d["oks"][c] > 0.0 for c in range(grp)]
            if d["restart"] is True:
                m_prev = floor
            else:
                m_prev = jnp.where(jnp.full((1, blk), d["restart"].astype(F32)) > 0.0, floor, m_old[h])
            m_new = m_prev
            for c in range(grp):
                cm = mxbuf[h, c, 0:1, :]
                m_new = jnp.maximum(m_new, cm if chosen[c] is None else jnp.where(chosen[c], cm, NEG_INF))
            for c in range(grp):
                mb = m_new if chosen[c] is None else jnp.where(chosen[c], m_new, jnp.inf)
                pbuf[h, c] = jnp.exp2(sbuf[h, c] - mb).astype(BF16)
            m_out.append(m_new)
            alpha_out.append(jnp.exp2(m_prev - m_new))
        return m_out, alpha_out

    def accumulate(pbuf, d, alpha):
        for h in heads:
            pv = None
            for c, j in enumerate(d["js"]):
                part = _dot(vt_ref[h, j], pbuf[h, c])
                pv = part if pv is None else pv + part
            acc_ref[d["slot"], h] = alpha[h] * acc_ref[d["slot"], h] + pv

    s_bufs, mx_bufs, p_bufs = (sa_ref, sb_ref), (ma_ref, mb_ref), (pa_ref, pb_ref)
    acc_ref[...] = jnp.zeros_like(acc_ref)
    groups = [group(g) for g in range(n_total)]
    m = [None for _ in heads]
    alpha = None
    scores_into(s_bufs[0], mx_bufs[0], groups[0])
    for g in range(n_total):
        if g + 1 < n_total:
            scores_into(s_bufs[(g + 1) % 2], mx_bufs[(g + 1) % 2], groups[g + 1])
        if g > 0:
            accumulate(p_bufs[(g - 1) % 2], groups[g - 1], alpha)
        m, alpha = softmax_into(p_bufs[g % 2], s_bufs[g % 2], mx_bufs[g % 2], groups[g], m)
    accumulate(p_bufs[(n_total - 1) % 2], groups[n_total - 1], alpha)
    for slot, out_ref in enumerate((oa_ref, ob_ref)):
        for h in heads:
            num = acc_ref[slot, h, :HEAD_DIM, :]
            den = acc_ref[slot, h, HEAD_DIM:HEAD_DIM + 1, :]
            out_ref[cols[h], :] = (num / den).astype(out_ref.dtype)


def _moba(z, bsz, seq, heads, q_col, k_col, v_col):
    blk = MOBA_BLOCK
    assert seq % (2 * blk) == 0 and MOBA_GROUP == 2
    nb = seq // blk
    half = nb // 2
    nbp = -(-nb // SUBLANES) * SUBLANES
    acc_rows = HEAD_DIM + MOBA_ONES_ROWS
    hp = math.gcd(heads, MOBA_HEADS_PER_STEP)
    assert all(col % hp == 0 for col in (q_col, k_col, v_col))
    width = hp * HEAD_DIM
    z3 = z.reshape(bsz, seq, z.shape[1])
    kern = functools.partial(_moba_kernel, nb=nb, nbp=nbp, log2_scale=HEAD_DIM ** -0.5 * LOG2_E, hp=hp,
                             q_col=q_col, k_col=k_col, v_col=v_col)
    seq_buf = pltpu.VMEM((seq, width), BF16)
    return pl.pallas_call(
        kern,
        grid=(bsz, heads // hp, half),
        in_specs=[pl.BlockSpec(memory_space=pl.ANY)],
        out_specs=[pl.BlockSpec((width, blk), lambda b, h, i: (h, b * half + i))] * 2,
        out_shape=[jax.ShapeDtypeStruct((heads * HEAD_DIM, bsz * seq // 2), BF16)] * 2,
        scratch_shapes=[seq_buf, seq_buf, seq_buf,
                        pltpu.SemaphoreType.DMA((3,)),
                        pltpu.VMEM((hp, nbp, HEAD_DIM), F32),
                        pltpu.VMEM((hp, nb, acc_rows, blk), BF16),
                        pltpu.VMEM((hp, nb, HEAD_DIM, blk), BF16),
                        pltpu.VMEM((hp, nb, nbp, blk), F32),
                        pltpu.VMEM((hp, MOBA_GROUP, blk, blk), F32),
                        pltpu.VMEM((hp, MOBA_GROUP, blk, blk), F32),
                        pltpu.VMEM((hp, MOBA_GROUP, SUBLANES, blk), F32),
                        pltpu.VMEM((hp, MOBA_GROUP, SUBLANES, blk), F32),
                        pltpu.VMEM((hp, MOBA_GROUP, blk, blk), BF16),
                        pltpu.VMEM((hp, MOBA_GROUP, blk, blk), BF16),
                        pltpu.VMEM((2, hp, acc_rows, blk), F32)],
        compiler_params=_params("parallel", "parallel", "arbitrary"),
        name="moba",
    )(z3)


def _hgrn_kernel(q_ref, f_ref, i_ref, og_ref, lb_ref, gain_ref, o_ref, st_ref, *, tt, chunk, sub, unroll):
    @pl.when(pl.program_id(2) == 0)
    def _():
        st_ref[...] = jnp.zeros_like(st_ref)

    n_blocks = chunk // sub
    stack = n_blocks * chunk
    lb = lb_ref[...]
    gain = gain_ref[...]
    tri = (lax.broadcasted_iota(jnp.int32, (chunk, chunk), 0)
           >= lax.broadcasted_iota(jnp.int32, (chunk, chunk), 1)).astype(BF16)
    r_i = lax.broadcasted_iota(jnp.int32, (chunk, stack), 0)
    c_i = lax.broadcasted_iota(jnp.int32, (chunk, stack), 1)
    keep = ((c_i // chunk) == (r_i // sub)) & ((c_i % chunk) <= r_i)
    row_c = lax.broadcasted_iota(jnp.int32, (chunk, HEAD_DIM), 0)

    def one_chunk(n, carry):
        off = pl.multiple_of(n * chunk, chunk)
        xq = q_ref[pl.ds(off, chunk), :].astype(F32)
        qs = xq * jax.nn.sigmoid(xq)
        f = lb + (1.0 - lb) * jax.nn.sigmoid(f_ref[pl.ds(off, chunk), :].astype(F32))
        logf = jnp.log(f)
        kk = 1.0 - f
        vb = i_ref[pl.ds(off, chunk), :]

        hi = logf.astype(BF16)
        r1 = logf - hi.astype(F32)
        mid = r1.astype(BF16)
        lo = (r1 - mid.astype(F32)).astype(BF16)
        b = _dot(tri, hi) + _dot(tri, mid) + _dot(tri, lo)
        b_last = b[chunk - 1:chunk, :]

        st = st_ref[...]
        qe = (qs * jnp.exp(b)).astype(BF16)
        o_inter = lax.dot_general(qe, st.astype(BF16), NT_DIMS, preferred_element_type=F32)

        ref_rows = jnp.concatenate(
            [jnp.broadcast_to(b[blk * sub:blk * sub + 1, :], (sub, HEAD_DIM)) for blk in range(n_blocks)],
            axis=0)
        qt = (qs * jnp.exp(b - ref_rows)).astype(BF16)
        kh = jnp.concatenate(
            [(kk * jnp.exp(jnp.where(row_c < (blk + 1) * sub, b[blk * sub:blk * sub + 1, :] - b, NEG_INF))
              ).astype(BF16) for blk in range(n_blocks)], axis=0)
        res = lax.dot_general(qt, kh, NT_DIMS, preferred_element_type=F32)
        a = jnp.where(keep, res, 0.0).astype(BF16)
        o = o_inter + _dot(a, jnp.concatenate([vb] * n_blocks, axis=0))

        kd = (kk * jnp.exp(b_last - b)).astype(BF16)
        st_ref[...] = st * jnp.exp(b_last) + lax.dot_general(vb, kd, TN_DIMS, preferred_element_type=F32)

        ms = jnp.mean(o * o, axis=-1, keepdims=True)
        on = o * lax.rsqrt(ms + RMS_EPS) * gain
        og = og_ref[pl.ds(off, chunk), :].astype(F32)
        o_ref[pl.ds(off, chunk), :] = (on * jax.nn.sigmoid(og)).astype(o_ref.dtype)
        return carry

    lax.fori_loop(0, tt // chunk, one_chunk, 0, unroll=unroll)


def _sigmoid(x):
    return 0.5 * jnp.tanh(0.5 * x) + 0.5


def _rows_to_blocks(rows, sub):
    return jnp.concatenate([jnp.broadcast_to(rows[k:k + 1, :], (sub, HEAD_DIM)) for k in range(rows.shape[0])],
                           axis=0)


def _hgrn_spans(q_ref, f_ref, i_ref, og_ref, lb_ref, gain_ref, o_ref, st_ref, *, n_spans, span, chunk, sub, hp,
                hooks=()):
    n_chunks = span // chunk
    nblk = chunk // sub
    n_all = span // sub
    stack = nblk * chunk
    r_s = lax.broadcasted_iota(jnp.int32, (span, span), 0)
    c_s = lax.broadcasted_iota(jnp.int32, (span, span), 1)
    tri = ((r_s >= c_s) & (r_s // chunk == c_s // chunk)).astype(BF16)
    r_i = lax.broadcasted_iota(jnp.int32, (chunk, stack), 0)
    c_i = lax.broadcasted_iota(jnp.int32, (chunk, stack), 1)
    keep = ((c_i // chunk) == (r_i // sub)) & ((c_i % chunk) <= r_i)
    heads = range(hp)
    cols = [slice(h * HEAD_DIM, (h + 1) * HEAD_DIM) for h in heads]
    f_mid = [0.5 + 0.5 * lb_ref[h] for h in heads]
    f_amp = [0.5 - 0.5 * lb_ref[h] for h in heads]

    def hook(k):
        if k < len(hooks):
            hooks[k]()

    def one_span(sidx, carry):
        rows = pl.ds(_aligned(sidx * span, span), span)

        qs, kk, vb, b = [], [], [], []
        for h in heads:
            xq = q_ref[rows, cols[h]].astype(F32)
            qs.append(xq * _sigmoid(xq))
            th = jnp.tanh(0.5 * f_ref[rows, cols[h]].astype(F32))
            logf = jnp.log(f_mid[h] + f_amp[h] * th)
            kk.append(f_amp[h] - f_amp[h] * th)
            vb.append(i_ref[rows, cols[h]])
            hi = logf.astype(BF16)
            r1 = logf - hi.astype(F32)
            mid = r1.astype(BF16)
            lo = (r1 - mid.astype(F32)).astype(BF16)
            b.append(_dot(tri, hi) + _dot(tri, mid) + _dot(tri, lo))

        hook(0)
        qe, b_last, res, upd = [], [], [], []
        for h in heads:
            first = jnp.concatenate([b[h][k * sub:k * sub + 1, :] for k in range(n_all)], axis=0)
            last = jnp.concatenate([b[h][k * sub + sub - 1:(k + 1) * sub, :] for k in range(n_all)], axis=0)
            qt = qs[h] * jnp.exp(b[h] - _rows_to_blocks(first, sub))
            kt = kk[h] * jnp.exp(_rows_to_blocks(last, sub) - b[h])
            bl = [last[(c + 1) * nblk - 1:(c + 1) * nblk, :] for c in range(n_chunks)]
            to_end = jnp.concatenate([bl[c] - last[c * nblk:(c + 1) * nblk, :] for c in range(n_chunks)], axis=0)
            qe.append((qt * _rows_to_blocks(jnp.exp(first), sub)).astype(BF16))
            kd = (kt * _rows_to_blocks(jnp.exp(to_end), sub)).astype(BF16)
            qtb = qt.astype(BF16)
            b_last.append(bl)
            res_h, upd_h = [], []
            for c in range(n_chunks):
                lo_r, hi_r = c * chunk, (c + 1) * chunk
                fr = first[c * nblk:(c + 1) * nblk, :]
                la = last[c * nblk:(c + 1) * nblk, :]
                g = jnp.exp(_rows_to_blocks(fr, nblk) - jnp.concatenate([la] * nblk, axis=0))
                ktc = kt[lo_r:hi_r, :]
                kh = jnp.concatenate(
                    [ktc[jb * sub:(jb + 1) * sub, :] * g[ib * nblk + jb:ib * nblk + jb + 1, :] if jb <= ib
                     else jnp.zeros((sub, HEAD_DIM), F32)
                     for ib in range(nblk) for jb in range(nblk)], axis=0).astype(BF16)
                res_h.append(lax.dot_general(qtb[lo_r:hi_r, :], kh, NT_DIMS, preferred_element_type=F32))
                upd_h.append(lax.dot_general(vb[h][lo_r:hi_r, :], kd[lo_r:hi_r, :], TN_DIMS,
                                             preferred_element_type=F32))
            res.append(res_h)
            upd.append(upd_h)

        hook(1)
        intra = [[_dot(jnp.where(keep, res[h][c], 0.0).astype(BF16),
                       jnp.concatenate([vb[h][c * chunk:(c + 1) * chunk, :]] * nblk, axis=0))
                  for c in range(n_chunks)] for h in heads]

        hook(2)
        st = [st_ref[h] for h in heads]
        for c in range(n_chunks):
            lo_r, hi_r = c * chunk, (c + 1) * chunk
            out_rows = pl.ds(_aligned(sidx * span + lo_r, chunk), chunk)
            for h in heads:
                o = intra[h][c] + lax.dot_general(qe[h][lo_r:hi_r, :], st[h].astype(BF16), NT_DIMS,
                                                  preferred_element_type=F32)
                st[h] = st[h] * jnp.exp(b_last[h][c]) + upd[h][c]
                ms = jnp.mean(o * o, axis=-1, keepdims=True)
                on = o * lax.rsqrt(ms + RMS_EPS) * gain_ref[h]
                og = og_ref[out_rows, cols[h]].astype(F32)
                o_ref[out_rows, cols[h]] = (on * _sigmoid(og)).astype(o_ref.dtype)
            hook(3 + c)
        for h in heads:
            st_ref[h] = st[h]
        return carry

    if n_spans == 1:
        one_span(0, 0)
    else:
        lax.fori_loop(0, n_spans, one_span, 0)


def _hgrn_safe(z, lb, gain, *, bsz, seq, heads, q_col, f_col, i_col, og_col):
    tt = _tile(seq, 256)
    nt = seq // tt
    kern = functools.partial(_hgrn_kernel, tt=tt, chunk=HGRN_SAFE_CHUNK, sub=1, unroll=False)

    def col_spec(col):
        return pl.BlockSpec((tt, HEAD_DIM), lambda b, h, t: (b * nt + t, col + h))

    head_spec = pl.BlockSpec((None, 1, HEAD_DIM), lambda b, h, t: (h, 0, 0))
    return pl.pallas_call(
        kern,
        grid=(bsz, heads, nt),
        in_specs=[col_spec(q_col), col_spec(f_col), col_spec(i_col), col_spec(og_col), head_spec, head_spec],
        out_specs=pl.BlockSpec((tt, HEAD_DIM), lambda b, h, t: (b * nt + t, h)),
        out_shape=jax.ShapeDtypeStruct((bsz * seq, heads * HEAD_DIM), BF16),
        scratch_shapes=[pltpu.VMEM((HEAD_DIM, HEAD_DIM), F32)],
        compiler_params=_params("parallel", "parallel", "arbitrary"),
        name="hgrn_safe",
    )(z, z, z, z, lb, gain)


def _hgrn_fast_ok(lb):
    return -(HGRN_SUB - 1) * jnp.log(jnp.min(lb)) <= HGRN_MAX_EXPONENT


def _mixout_kernel(*refs, alpha, n_gate, gw, nb, hgrn):
    if hgrn:
        (q_ref, f_ref, i_ref, og_ref, lb_ref, gain_ref), refs = refs[:6], refs[6:]
    (oa_lo_ref, oa_hi_ref), refs = refs[:2], refs[2:]
    if not hgrn:
        ob_in_ref, refs = refs[0], refs[1:]
    ga_refs, gb_refs, refs = refs[:n_gate], refs[n_gate:2 * n_gate], refs[2 * n_gate:]
    x_ref, wpa_ref, wpb_ref, wo_ref, g_ref, b_ref, o_ref = refs[:7]
    step = pl.program_id(0)
    if hgrn:
        st_ref, ob_buf_ref = refs[7:]
        tile = jnp.maximum(step - 1, 0)
        next_tile = jnp.minimum(step, pl.num_programs(0) - 2)
        slot = step % 2

        @pl.when(step == 0)
        def _():
            ob_buf_ref[...] = jnp.zeros_like(ob_buf_ref)

        @pl.when(next_tile % nb == 0)
        def _():
            st_ref[...] = jnp.zeros_like(st_ref)

        ob = ob_buf_ref[1 - slot]
    else:
        tile = step
        ob = ob_in_ref[...]
    in_lo = tile % nb < nb // 2
    oa_t = jnp.where(in_lo, oa_lo_ref[...], oa_hi_ref[...])
    merged, parts = {}, []

    def project(c):
        ya = lax.dot_general(oa_t, wpa_ref[:, c * gw:(c + 1) * gw], TN_DIMS, preferred_element_type=F32)
        yb = _dot(ob, wpb_ref[:, c * gw:(c + 1) * gw])
        merged[c] = (jax.nn.sigmoid(ga_refs[c][...].astype(F32)) * ya
                     + jax.nn.sigmoid(gb_refs[c][...].astype(F32)) * yb).astype(BF16)

    def output(c):
        parts.append(_dot(merged[c], wo_ref[c * gw:(c + 1) * gw, :]))

    def finish():
        o_ref[...] = _layer_norm(alpha * x_ref[...] + sum(parts[1:], parts[0]), g_ref[...], b_ref[...])

    def chain(*fs):
        return lambda: [f() for f in fs]

    steps = [functools.partial(project, 0)]
    steps += [chain(functools.partial(output, c - 1), functools.partial(project, c)) for c in range(1, n_gate)]
    pieces = steps + [functools.partial(output, n_gate - 1), finish]
    if hgrn:
        n_hooks = 3 + ob_buf_ref.shape[1] // hgrn["chunk"]
        _hgrn_spans(q_ref, f_ref, i_ref, og_ref, lb_ref, gain_ref, ob_buf_ref.at[slot], st_ref, n_spans=1,
                    span=ob_buf_ref.shape[1], hooks=pieces[:n_hooks], **hgrn)
        pieces = pieces[n_hooks:]
    for piece in pieces:
        piece()


def _mixout(o_a_halves, o_b, z, x, wpa, wpb, wo, ln_g, ln_b, *, alpha, ga_off, gb_off, nb, hgrn=None):
    m, d = x.shape
    gw = math.gcd(math.gcd(ga_off, gb_off), d)
    n_gate = d // gw
    tm = MOBA_BLOCK
    half = nb // 2
    n_tiles = m // tm
    if hgrn:
        hb = hgrn["heads"]
        b_width = hb * HEAD_DIM
        assert tm % HGRN_CHUNK == 0 and all(hgrn[k] % hb == 0 for k in ("q_col", "f_col", "i_col", "og_col"))
        mix_tile = lambda i: jnp.maximum(i - 1, 0)
        next_tile = lambda i: jnp.minimum(i, n_tiles - 1)
        settings = dict(chunk=HGRN_CHUNK, sub=HGRN_SUB, hp=hb)
    else:
        mix_tile = lambda i: i
        settings = None
    kern = functools.partial(_mixout_kernel, alpha=alpha, n_gate=n_gate, gw=gw, nb=nb, hgrn=settings)

    def gate_spec(off, c):
        return pl.BlockSpec((tm, gw), lambda i: (mix_tile(i), off // gw + c))

    def whole(a):
        return pl.BlockSpec(a.shape, lambda i: (0,) * a.ndim, pipeline_mode=pl.Buffered(1))

    def rows(a):
        return pl.BlockSpec((tm, a.shape[1]), lambda i: (mix_tile(i), 0))

    def attn_spec(pos):
        def index(i):
            t = mix_tile(i)
            return 0, (t // nb) * half + pos(t % nb)
        return pl.BlockSpec((o_a_halves[0].shape[0], tm), index)

    attn_specs = [attn_spec(lambda blk: jnp.minimum(blk, half - 1)),
                  attn_spec(lambda blk: nb - 1 - jnp.maximum(blk, half))]
    tail_specs = ([gate_spec(ga_off, c) for c in range(n_gate)] + [gate_spec(gb_off, c) for c in range(n_gate)]
                  + [rows(x), whole(wpa), whole(wpb), whole(wo), whole(ln_g), whole(ln_b)])
    tail_args = (*([z] * (2 * n_gate)), x, wpa, wpb, wo, ln_g, ln_b)
    if hgrn:
        def col_spec(col):
            return pl.BlockSpec((tm, b_width), lambda i: (next_tile(i), col // hb))
        in_specs = ([col_spec(hgrn[k]) for k in ("q_col", "f_col", "i_col", "og_col")]
                    + [whole(hgrn["lb"]), whole(hgrn["gain"])] + attn_specs + tail_specs)
        args = (z, z, z, z, hgrn["lb"], hgrn["gain"], *o_a_halves, *tail_args)
        scratch = [pltpu.VMEM((hb, HEAD_DIM, HEAD_DIM), F32),
                   pltpu.VMEM((2, tm, b_width), BF16)]
    else:
        in_specs = attn_specs + [rows(o_b)] + tail_specs
        args = (*o_a_halves, o_b, *tail_args)
        scratch = []
    return pl.pallas_call(
        kern,
        grid=(n_tiles + (1 if hgrn else 0),),
        in_specs=in_specs,
        out_specs=pl.BlockSpec((tm, d), lambda i: (mix_tile(i), 0)),
        out_shape=jax.ShapeDtypeStruct((m, d), F32),
        scratch_shapes=scratch,
        compiler_params=_params("arbitrary"),
        name="hgrn_mixout" if hgrn else "mixout",
    )(*args)


def _ffn_kernel(h_ref, wg_ref, wu_ref, wd_ref, g_ref, b_ref, o_ref, hb_ref, *, alpha, nf):
    f = pl.program_id(1)
    chunk = min(FFN_ROW_CHUNK, hb_ref.shape[0])

    def step(first, last):
        for r in range(0, hb_ref.shape[0], chunk):
            rows = slice(r, r + chunk)
            if first:
                hb = h_ref[rows, :].astype(BF16)
                hb_ref[rows, :] = hb
            else:
                hb = hb_ref[rows, :]
            gate = _dot(hb, wg_ref[...])
            up = _dot(hb, wu_ref[...])
            act = (gate * _sigmoid(gate) * up).astype(BF16)
            down = _dot(act, wd_ref[...])
            acc = down if first else o_ref[rows, :] + down
            o_ref[rows, :] = _layer_norm(alpha * h_ref[rows, :] + acc, g_ref[...], b_ref[...]) if last else acc

    if nf == 1:
        step(True, True)
    else:
        pl.when(f == 0)(functools.partial(step, True, False))
        pl.when(f == nf - 1)(functools.partial(step, False, True))
        if nf > 2:
            pl.when((f > 0) & (f < nf - 1))(functools.partial(step, False, False))


def _ffn(h, wg, wu, wd, ln_g, ln_b, *, alpha):
    m, d = h.shape
    hidden = wg.shape[1]
    tm, tf = _tile(m, 1024), _tile(hidden, 512)
    assert tm % min(FFN_ROW_CHUNK, tm) == 0
    nf = hidden // tf
    kern = functools.partial(_ffn_kernel, alpha=alpha, nf=nf)
    vec = pl.BlockSpec((1, d), lambda i, f: (0, 0))
    return pl.pallas_call(
        kern,
        grid=(m // tm, nf),
        in_specs=[pl.BlockSpec((tm, d), lambda i, f: (i, 0)),
                  pl.BlockSpec((d, tf), lambda i, f: (0, f)),
                  pl.BlockSpec((d, tf), lambda i, f: (0, f)),
                  pl.BlockSpec((tf, d), lambda i, f: (f, 0)),
                  vec, vec],
        out_specs=pl.BlockSpec((tm, d), lambda i, f: (i, 0)),
        out_shape=jax.ShapeDtypeStruct((m, d), F32),
        scratch_shapes=[pltpu.VMEM((tm, d), BF16)],
        compiler_params=_params("parallel", "arbitrary"),
        name="ffn",
    )(h, wg, wu, wd, ln_g, ln_b)


def kernel(x, w_in, w_proj_a, w_proj_b, w_out, hgrn_norm_g, hgrn_lb_logits, ln1_g, ln1_b,
           w_gate_ffn, w_up_ffn, w_down_ffn, ln2_g, ln2_b):
    bsz, seq, d = x.shape
    depth = w_in.shape[0]
    a_width = w_proj_a.shape[1]
    b_vwidth = w_proj_b.shape[1]
    b_fwidth = hgrn_lb_logits.shape[1]
    assert a_width % HEAD_DIM == 0 and b_vwidth == b_fwidth and b_fwidth % HEAD_DIM == 0
    assert w_in.shape[2] == 3 * a_width + 2 * b_fwidth + 2 * b_vwidth + 2 * d
    a_heads = a_width // HEAD_DIM
    b_heads = b_fwidth // HEAD_DIM
    qa_col, ka_col, va_col = 0, a_heads, 2 * a_heads
    qb_col = 3 * a_heads
    fb_col, ib_col, ogb_col = qb_col + b_heads, qb_col + 2 * b_heads, qb_col + 3 * b_heads
    ga_off = 3 * a_width + 2 * b_fwidth + 2 * b_vwidth
    gb_off = ga_off + d
    alpha = (2.0 * depth) ** 0.25

    lb_all = jnp.cumsum(jax.nn.softmax(hgrn_lb_logits.astype(F32), axis=0), axis=0)

    h = x.reshape(bsz * seq, d)
    for l in range(depth):
        z, (wg, wu, wd) = _inproj(h, w_in[l].astype(BF16), cast=(w_gate_ffn[l], w_up_ffn[l], w_down_ffn[l]))
        o_a = _moba(z, bsz, seq, a_heads, qa_col, ka_col, va_col)
        lb = lb_all[l].reshape(b_heads, 1, HEAD_DIM)
        gain = hgrn_norm_g[l].astype(F32).reshape(b_heads, 1, HEAD_DIM)
        b_cols = dict(heads=b_heads, q_col=qb_col, f_col=fb_col, i_col=ib_col, og_col=ogb_col)
        mix = functools.partial(_mixout, alpha=alpha, ga_off=ga_off, gb_off=gb_off, nb=seq // MOBA_BLOCK)
        mix_weights = (w_proj_a[l].astype(BF16), w_proj_b[l].astype(BF16), w_out[l].astype(BF16),
                       ln1_g[l].reshape(1, d), ln1_b[l].reshape(1, d))

        def fused(o_a, z, h, lb, gain, *w):
            return mix(o_a, None, z, h, *w, hgrn=dict(lb=lb, gain=gain, **b_cols))

        def split(o_a, z, h, lb, gain, *w):
            o_b = _hgrn_safe(z, lb, gain, bsz=bsz, seq=seq, **b_cols)
            return mix(o_a, o_b, z, h, *w)

        h = lax.cond(_hgrn_fast_ok(lb), fused, split, o_a, z, h, lb, gain, *mix_weights)
        h = _ffn(h, wg, wu, wd, ln2_g[l].reshape(1, d), ln2_b[l].reshape(1, d), alpha=alpha)
    return h.reshape(bsz, seq, d)
```

```python
import functools
import math

import jax
import jax.numpy as jnp
from jax import lax
from jax.experimental import pallas as pl
from jax.experimental.pallas import tpu as pltpu

F32 = jnp.float32
BF16 = jnp.bfloat16

LANES = 128
SUBLANES = 8
HEAD_DIM = 128
MOBA_BLOCK = 256
MOBA_TOPK = 3
MOBA_GROUP = 2
MOBA_HEADS_PER_STEP = 2
MOBA_MAX_FLOOR = float(jnp.finfo(jnp.float32).min)
MOBA_ONES_ROWS = 16
LOG2_E = 1.4426950408889634
LN_EPS = 1e-5
RMS_EPS = 1e-6
NEG_INF = float("-inf")

HGRN_CHUNK = 64
HGRN_SUB = 16
HGRN_SAFE_CHUNK = 8
HGRN_MAX_EXPONENT = 60.0

VMEM_LIMIT_BYTES = 56 * 1024 * 1024
FFN_ROW_CHUNK = 512

NT_DIMS = (((1,), (1,)), ((), ()))
TN_DIMS = (((0,), (0,)), ((), ()))


def _tile(n, pref):
    if n <= pref:
        return n
    t = pref - pref % LANES
    while t >= LANES:
        if n % t == 0:
            return t
        t -= LANES
    raise ValueError(f"no lane-aligned tile for {n}")


def _params(*sem):
    return pltpu.CompilerParams(dimension_semantics=sem, vmem_limit_bytes=VMEM_LIMIT_BYTES)


def _aligned(offset, multiple):
    return offset if isinstance(offset, int) else pl.multiple_of(offset, multiple)


def _dot(a, b):
    return jnp.dot(a, b, preferred_element_type=F32)


def _layer_norm(pre, g, b):
    mu = jnp.mean(pre, axis=-1, keepdims=True)
    d = pre - mu
    var = jnp.mean(d * d, axis=-1, keepdims=True)
    return d * lax.rsqrt(var + LN_EPS) * g + b


def _inproj_kernel(x_ref, w_ref, o_ref, xb_ref):
    @pl.when(pl.program_id(1) == 0)
    def _():
        xb_ref[...] = x_ref[...].astype(BF16)

    o_ref[...] = _dot(xb_ref[...], w_ref[...]).astype(o_ref.dtype)


def _inproj(x, w):
    m, k = x.shape
    n = w.shape[1]
    tm, tn = _tile(m, 1024), _tile(n, 1024)
    return pl.pallas_call(
        _inproj_kernel,
        grid=(m // tm, n // tn),
        in_specs=[pl.BlockSpec((tm, k), lambda i, j: (i, 0)),
                  pl.BlockSpec((k, tn), lambda i, j: (0, j))],
        out_specs=pl.BlockSpec((tm, tn), lambda i, j: (i, j)),
        out_shape=jax.ShapeDtypeStruct((m, n), BF16),
        scratch_shapes=[pltpu.VMEM((tm, k), BF16)],
        compiler_params=_params("parallel", "arbitrary"),
        name="inproj",
    )(x, w)


def _moba_kernel(z_hbm, oa_ref, ob_ref, q_ref, k_ref, v_ref, dma_sem, kmean_ref, vt_ref, qts_ref, sel_ref,
                 sa_ref, sb_ref, ma_ref, mb_ref, pa_ref, pb_ref, acc_ref,
                 *, nb, nbp, log2_scale, hp, q_col, k_col, v_col):
    i = pl.program_id(2)
    blk = MOBA_BLOCK
    grp = MOBA_GROUP
    heads = range(hp)
    cols = [slice(h * HEAD_DIM, (h + 1) * HEAD_DIM) for h in heads]

    @pl.when(i == 0)
    def _():
        width = hp * HEAD_DIM
        copies = [
            pltpu.make_async_copy(
                z_hbm.at[pl.program_id(0), :, pl.ds((col // hp + pl.program_id(1)) * width, width)], dst,
                dma_sem.at[idx])
            for idx, (col, dst) in enumerate(((k_col, k_ref), (v_col, v_ref), (q_col, q_ref)))]
        for cp in copies:
            cp.start()
        copies[0].wait()
        kmean_ref[...] = jnp.zeros_like(kmean_ref)

        def key_means(n, carry):
            rows = pl.ds(pl.multiple_of(n * blk, blk), blk)
            for h in heads:
                kf = k_ref[rows, cols[h]].astype(F32)
                kmean_ref[h, pl.ds(n, 1), :] = jnp.sum(kf, axis=0, keepdims=True) * (1.0 / blk)
            return carry

        lax.fori_loop(0, nb, key_means, 0)
        copies[1].wait()

        def values(n, carry):
            rows = pl.ds(pl.multiple_of(n * blk, blk), blk)
            for h in heads:
                vt_ref[h, n, :HEAD_DIM, :] = v_ref[rows, cols[h]].astype(F32).T.astype(BF16)
                vt_ref[h, n, HEAD_DIM:, :] = jnp.ones((MOBA_ONES_ROWS, blk), BF16)
            return carry

        lax.fori_loop(0, nb, values, 0, unroll=2 if nb % 2 == 0 else 1)
        copies[2].wait()
        row = lax.broadcasted_iota(jnp.int32, (nbp, blk), 0)

        def select(n, carry):
            rows = pl.ds(pl.multiple_of(n * blk, blk), blk)
            for h in heads:
                qt32 = q_ref[rows, cols[h]].astype(F32).T
                qt = qt32.astype(BF16)
                qts_ref[h, n] = (qt32 * log2_scale).astype(BF16)
                km = kmean_ref[h]
                km_hi = km.astype(BF16)
                km_lo = (km - km_hi.astype(F32)).astype(BF16)
                gate = _dot(km_hi, qt) + _dot(km_lo, qt)
                g = jnp.where(row < n, gate, NEG_INF)
                sel = jnp.zeros((nbp, blk), dtype=jnp.bool_)
                for _ in range(MOBA_TOPK):
                    mx = jnp.max(g, axis=0, keepdims=True)
                    idx = jnp.min(jnp.where(g == mx, row, nbp), axis=0, keepdims=True)
                    pick = (row == idx) & (mx > NEG_INF)
                    sel = sel | pick
                    g = jnp.where(pick, NEG_INF, g)
                sel_ref[h, n] = sel.astype(F32)
            return carry

        lax.fori_loop(0, nb, select, 0, unroll=4 if nb % 4 == 0 else 1)

    blk_a, blk_b = i, nb - 1 - i
    n_a = (blk_a + grp) // grp
    n_total = nb // grp + 1

    def group(g):
        in_b = g >= n_a
        qb = jnp.where(in_b, blk_b, blk_a)
        n = jnp.where(in_b, n_total - 1 - g, g)
        ids = [qb - n * grp - c for c in range(grp)]
        return dict(qb=qb, slot=in_b.astype(jnp.int32), own=g in (0, n_total - 1),
                    restart=True if g == 0 else g == n_a,
                    js=[jnp.maximum(j, 0) for j in ids], oks=[(j >= 0).astype(F32) for j in ids])

    def scores_into(sbuf, mxbuf, d):
        for h in heads:
            for c, j in enumerate(d["js"]):
                s = _dot(k_ref[pl.ds(pl.multiple_of(j * blk, blk), blk), cols[h]], qts_ref[h, d["qb"]])
                if d["own"] and c == 0:
                    s = jnp.where(lax.broadcasted_iota(jnp.int32, (blk, blk), 0)
                                  <= lax.broadcasted_iota(jnp.int32, (blk, blk), 1), s, NEG_INF)
                sbuf[h, c] = s
                mxbuf[h, c] = jnp.broadcast_to(jnp.max(s, axis=0, keepdims=True), (SUBLANES, blk))

    def softmax_into(pbuf, sbuf, mxbuf, d, m_old):
        floor = jnp.full((1, blk), MOBA_MAX_FLOOR, F32)
        m_out, alpha_out = [], []
        for h in heads:
            chosen = [None if (d["own"] and c == 0)
                      else sel_ref[h, d["qb"], pl.ds(d["js"][c], 1), :] * d["oks"][c] > 0.0 for c in range(grp)]
            if d["restart"] is True:
                m_prev = floor
            else:
                m_prev = jnp.where(jnp.full((1, blk), d["restart"].astype(F32)) > 0.0, floor, m_old[h])
            m_new = m_prev
            for c in range(grp):
                cm = mxbuf[h, c, 0:1, :]
                m_new = jnp.maximum(m_new, cm if chosen[c] is None else jnp.where(chosen[c], cm, NEG_INF))
            for c in range(grp):
                mb = m_new if chosen[c] is None else jnp.where(chosen[c], m_new, jnp.inf)
                pbuf[h, c] = jnp.exp2(sbuf[h, c] - mb).astype(BF16)
            m_out.append(m_new)
            alpha_out.append(jnp.exp2(m_prev - m_new))
        return m_out, alpha_out

    def accumulate(pbuf, d, alpha):
        for h in heads:
            pv = None
            for c, j in enumerate(d["js"]):
                part = _dot(vt_ref[h, j], pbuf[h, c])
                pv = part if pv is None else pv + part
            acc_ref[d["slot"], h] = alpha[h] * acc_ref[d["slot"], h] + pv

    s_bufs, mx_bufs, p_bufs = (sa_ref, sb_ref), (ma_ref, mb_ref), (pa_ref, pb_ref)
    acc_ref[...] = jnp.zeros_like(acc_ref)
    groups = [group(g) for g in range(n_total)]
    m = [None for _ in heads]
    alpha = None
    scores_into(s_bufs[0], mx_bufs[0], groups[0])
    for g in range(n_total):
        if g + 1 < n_total:
            scores_into(s_bufs[(g + 1) % 2], mx_bufs[(g + 1) % 2], groups[g + 1])
        if g > 0:
            accumulate(p_bufs[(g - 1) % 2], groups[g - 1], alpha)
        m, alpha = softmax_into(p_bufs[g % 2], s_bufs[g % 2], mx_bufs[g % 2], groups[g], m)
    accumulate(p_bufs[(n_total - 1) % 2], groups[n_total - 1], alpha)
    for slot, out_ref in enumerate((oa_ref, ob_ref)):
        for h in heads:
            num = acc_ref[slot, h, :HEAD_DIM, :]
            den = acc_ref[slot, h, HEAD_DIM:HEAD_DIM + 1, :]
            out_ref[cols[h], :] = (num / den).astype(out_ref.dtype)


def _moba(z, bsz, seq, heads, q_col, k_col, v_col):
    blk = MOBA_BLOCK
    assert seq % (2 * blk) == 0 and MOBA_GROUP == 2
    nb = seq // blk
    half = nb // 2
    nbp = -(-nb // SUBLANES) * SUBLANES
    acc_rows = HEAD_DIM + MOBA_ONES_ROWS
    hp = math.gcd(heads, MOBA_HEADS_PER_STEP)
    assert all(col % hp == 0 for col in (q_col, k_col, v_col))
    width = hp * HEAD_DIM
    z3 = z.reshape(bsz, seq, z.shape[1])
    kern = functools.partial(_moba_kernel, nb=nb, nbp=nbp, log2_scale=HEAD_DIM ** -0.5 * LOG2_E, hp=hp,
                             q_col=q_col, k_col=k_col, v_col=v_col)
    seq_buf = pltpu.VMEM((seq, width), BF16)
    return pl.pallas_call(
        kern,
        grid=(bsz, heads // hp, half),
        in_specs=[pl.BlockSpec(memory_space=pl.ANY)],
        out_specs=[pl.BlockSpec((width, blk), lambda b, h, i: (h, b * half + i))] * 2,
        out_shape=[jax.ShapeDtypeStruct((heads * HEAD_DIM, bsz * seq // 2), BF16)] * 2,
        scratch_shapes=[seq_buf, seq_buf, seq_buf,
                        pltpu.SemaphoreType.DMA((3,)),
                        pltpu.VMEM((hp, nbp, HEAD_DIM), F32),
                        pltpu.VMEM((hp, nb, acc_rows, blk), BF16),
                        pltpu.VMEM((hp, nb, HEAD_DIM, blk), BF16),
                        pltpu.VMEM((hp, nb, nbp, blk), F32),
                        pltpu.VMEM((hp, MOBA_GROUP, blk, blk), F32),
                        pltpu.VMEM((hp, MOBA_GROUP, blk, blk), F32),
                        pltpu.VMEM((hp, MOBA_GROUP, SUBLANES, blk), F32),
                        pltpu.VMEM((hp, MOBA_GROUP, SUBLANES, blk), F32),
                        pltpu.VMEM((hp, MOBA_GROUP, blk, blk), BF16),
                        pltpu.VMEM((hp, MOBA_GROUP, blk, blk), BF16),
                        pltpu.VMEM((2, hp, acc_rows, blk), F32)],
        compiler_params=_params("parallel", "parallel", "arbitrary"),
        name="moba",
    )(z3)


def _hgrn_kernel(q_ref, f_ref, i_ref, og_ref, lb_ref, gain_ref, o_ref, st_ref, *, tt, chunk, sub, unroll):
    @pl.when(pl.program_id(2) == 0)
    def _():
        st_ref[...] = jnp.zeros_like(st_ref)

    n_blocks = chunk // sub
    stack = n_blocks * chunk
    lb = lb_ref[...]
    gain = gain_ref[...]
    tri = (lax.broadcasted_iota(jnp.int32, (chunk, chunk), 0)
           >= lax.broadcasted_iota(jnp.int32, (chunk, chunk), 1)).astype(BF16)
    r_i = lax.broadcasted_iota(jnp.int32, (chunk, stack), 0)
    c_i = lax.broadcasted_iota(jnp.int32, (chunk, stack), 1)
    keep = ((c_i // chunk) == (r_i // sub)) & ((c_i % chunk) <= r_i)
    row_c = lax.broadcasted_iota(jnp.int32, (chunk, HEAD_DIM), 0)

    def one_chunk(n, carry):
        off = pl.multiple_of(n * chunk, chunk)
        xq = q_ref[pl.ds(off, chunk), :].astype(F32)
        qs = xq * jax.nn.sigmoid(xq)
        f = lb + (1.0 - lb) * jax.nn.sigmoid(f_ref[pl.ds(off, chunk), :].astype(F32))
        logf = jnp.log(f)
        kk = 1.0 - f
        vb = i_ref[pl.ds(off, chunk), :]

        hi = logf.astype(BF16)
        r1 = logf - hi.astype(F32)
        mid = r1.astype(BF16)
        lo = (r1 - mid.astype(F32)).astype(BF16)
        b = _dot(tri, hi) + _dot(tri, mid) + _dot(tri, lo)
        b_last = b[chunk - 1:chunk, :]

        st = st_ref[...]
        qe = (qs * jnp.exp(b)).astype(BF16)
        o_inter = lax.dot_general(qe, st.astype(BF16), NT_DIMS, preferred_element_type=F32)

        ref_rows = jnp.concatenate(
            [jnp.broadcast_to(b[blk * sub:blk * sub + 1, :], (sub, HEAD_DIM)) for blk in range(n_blocks)],
            axis=0)
        qt = (qs * jnp.exp(b - ref_rows)).astype(BF16)
        kh = jnp.concatenate(
            [(kk * jnp.exp(jnp.where(row_c < (blk + 1) * sub, b[blk * sub:blk * sub + 1, :] - b, NEG_INF))
              ).astype(BF16) for blk in range(n_blocks)], axis=0)
        res = lax.dot_general(qt, kh, NT_DIMS, preferred_element_type=F32)
        a = jnp.where(keep, res, 0.0).astype(BF16)
        o = o_inter + _dot(a, jnp.concatenate([vb] * n_blocks, axis=0))

        kd = (kk * jnp.exp(b_last - b)).astype(BF16)
        st_ref[...] = st * jnp.exp(b_last) + lax.dot_general(vb, kd, TN_DIMS, preferred_element_type=F32)

        ms = jnp.mean(o * o, axis=-1, keepdims=True)
        on = o * lax.rsqrt(ms + RMS_EPS) * gain
        og = og_ref[pl.ds(off, chunk), :].astype(F32)
        o_ref[pl.ds(off, chunk), :] = (on * jax.nn.sigmoid(og)).astype(o_ref.dtype)
        return carry

    lax.fori_loop(0, tt // chunk, one_chunk, 0, unroll=unroll)


def _sigmoid(x):
    return 0.5 * jnp.tanh(0.5 * x) + 0.5


def _chunk_cumsum(x, chunk):
    row_in_chunk = lax.broadcasted_iota(jnp.int32, x.shape, 0) % chunk
    shift = 1
    while shift < chunk:
        x = x + jnp.where(row_in_chunk >= shift, pltpu.roll(x, shift, axis=0), 0.0)
        shift *= 2
    return x


def _rows_to_blocks(rows, sub):
    return jnp.concatenate([jnp.broadcast_to(rows[k:k + 1, :], (sub, HEAD_DIM)) for k in range(rows.shape[0])],
                           axis=0)


def _hgrn_spans(q_ref, f_ref, i_ref, og_ref, lb_ref, gain_ref, o_ref, st_ref, *, n_spans, span, chunk, sub, hp,
                hooks=(), as_generator=False):
    n_chunks = span // chunk
    nblk = chunk // sub
    n_all = span // sub
    stack = nblk * chunk
    r_i = lax.broadcasted_iota(jnp.int32, (chunk, stack), 0)
    c_i = lax.broadcasted_iota(jnp.int32, (chunk, stack), 1)
    keep = ((c_i // chunk) == (r_i // sub)) & ((c_i % chunk) <= r_i)
    heads = range(hp)
    cols = [slice(h * HEAD_DIM, (h + 1) * HEAD_DIM) for h in heads]
    f_mid = [0.5 + 0.5 * lb_ref[h] for h in heads]
    f_amp = [0.5 - 0.5 * lb_ref[h] for h in heads]

    def hook(k):
        if k < len(hooks):
            hooks[k]()

    def one_span(sidx, carry):
        rows = pl.ds(_aligned(sidx * span, span), span)

        qs, kk, vb, b = [], [], [], []
        for h in heads:
            xq = q_ref[rows, cols[h]].astype(F32)
            qs.append(xq * _sigmoid(xq))
            th = jnp.tanh(0.5 * f_ref[rows, cols[h]].astype(F32))
            logf = jnp.log(f_mid[h] + f_amp[h] * th)
            kk.append(f_amp[h] - f_amp[h] * th)
            vb.append(i_ref[rows, cols[h]])
            b.append(_chunk_cumsum(logf, chunk))

        yield
        qe, b_last, res, upd = [], [], [], []
        for h in heads:
            first = jnp.concatenate([b[h][k * sub:k * sub + 1, :] for k in range(n_all)], axis=0)
            last = jnp.concatenate([b[h][k * sub + sub - 1:(k + 1) * sub, :] for k in range(n_all)], axis=0)
            qt = qs[h] * jnp.exp(b[h] - _rows_to_blocks(first, sub))
            kt = kk[h] * jnp.exp(_rows_to_blocks(last, sub) - b[h])
            bl = [last[(c + 1) * nblk - 1:(c + 1) * nblk, :] for c in range(n_chunks)]
            to_end = jnp.concatenate([bl[c] - last[c * nblk:(c + 1) * nblk, :] for c in range(n_chunks)], axis=0)
            qe.append((qt * _rows_to_blocks(jnp.exp(first), sub)).astype(BF16))
            kd = (kt * _rows_to_blocks(jnp.exp(to_end), sub)).astype(BF16)
            qtb = qt.astype(BF16)
            b_last.append(bl)
            res_h, upd_h = [], []
            for c in range(n_chunks):
                lo_r, hi_r = c * chunk, (c + 1) * chunk
                fr = first[c * nblk:(c + 1) * nblk, :]
                la = last[c * nblk:(c + 1) * nblk, :]
                g = jnp.exp(_rows_to_blocks(fr, nblk) - jnp.concatenate([la] * nblk, axis=0))
                ktc = kt[lo_r:hi_r, :]
                kh = jnp.concatenate(
                    [ktc[jb * sub:(jb + 1) * sub, :] * g[ib * nblk + jb:ib * nblk + jb + 1, :] if jb <= ib
                     else jnp.zeros((sub, HEAD_DIM), F32)
                     for ib in range(nblk) for jb in range(nblk)], axis=0).astype(BF16)
                res_h.append(lax.dot_general(qtb[lo_r:hi_r, :], kh, NT_DIMS, preferred_element_type=F32))
                upd_h.append(lax.dot_general(vb[h][lo_r:hi_r, :], kd[lo_r:hi_r, :], TN_DIMS,
                                             preferred_element_type=F32))
            res.append(res_h)
            upd.append(upd_h)

        yield
        intra = [[_dot(jnp.where(keep, res[h][c], 0.0).astype(BF16),
                       jnp.concatenate([vb[h][c * chunk:(c + 1) * chunk, :]] * nblk, axis=0))
                  for c in range(n_chunks)] for h in heads]

        yield
        st = [st_ref[h] for h in heads]
        for c in range(n_chunks):
            lo_r, hi_r = c * chunk, (c + 1) * chunk
            out_rows = pl.ds(_aligned(sidx * span + lo_r, chunk), chunk)
            for h in heads:
                o = intra[h][c] + lax.dot_general(qe[h][lo_r:hi_r, :], st[h].astype(BF16), NT_DIMS,
                                                  preferred_element_type=F32)
                st[h] = st[h] * jnp.exp(b_last[h][c]) + upd[h][c]
                ms = jnp.mean(o * o, axis=-1, keepdims=True)
                on = o * lax.rsqrt(ms + RMS_EPS) * gain_ref[h]
                og = og_ref[out_rows, cols[h]].astype(F32)
                o_ref[out_rows, cols[h]] = (on * _sigmoid(og)).astype(o_ref.dtype)
            yield
        for h in heads:
            st_ref[h] = st[h]
        return carry

    assert n_spans == 1
    stages = one_span(0, 0)
    if as_generator:
        return stages
    for k, _ in enumerate(stages):
        hook(k)


def _hgrn_safe(z, lb, gain, *, bsz, seq, heads, q_col, f_col, i_col, og_col):
    tt = _tile(seq, 256)
    nt = seq // tt
    kern = functools.partial(_hgrn_kernel, tt=tt, chunk=HGRN_SAFE_CHUNK, sub=1, unroll=False)

    def col_spec(col):
        return pl.BlockSpec((tt, HEAD_DIM), lambda b, h, t: (b * nt + t, col + h))

    head_spec = pl.BlockSpec((None, 1, HEAD_DIM), lambda b, h, t: (h, 0, 0))
    return pl.pallas_call(
        kern,
        grid=(bsz, heads, nt),
        in_specs=[col_spec(q_col), col_spec(f_col), col_spec(i_col), col_spec(og_col), head_spec, head_spec],
        out_specs=pl.BlockSpec((tt, HEAD_DIM), lambda b, h, t: (b * nt + t, h)),
        out_shape=jax.ShapeDtypeStruct((bsz * seq, heads * HEAD_DIM), BF16),
        scratch_shapes=[pltpu.VMEM((HEAD_DIM, HEAD_DIM), F32)],
        compiler_params=_params("parallel", "parallel", "arbitrary"),
        name="hgrn_safe",
    )(z, z, z, z, lb, gain)


def _hgrn_fast_ok(lb):
    return -(HGRN_SUB - 1) * jnp.log(jnp.min(lb)) <= HGRN_MAX_EXPONENT


def _mixout_kernel(*refs, alpha, n_gate, gw, nb, hgrn):
    if hgrn:
        (q_ref, f_ref, i_ref, og_ref, lb_ref, gain_ref), refs = refs[:6], refs[6:]
    (oa_lo_ref, oa_hi_ref), refs = refs[:2], refs[2:]
    if not hgrn:
        ob_in_ref, refs = refs[0], refs[1:]
    ga_refs, gb_refs, refs = refs[:n_gate], refs[n_gate:2 * n_gate], refs[2 * n_gate:]
    x_ref, wpa_ref, wpb_ref, wo_ref, g_ref, b_ref, o_ref = refs[:7]
    step = pl.program_id(0)
    if hgrn:
        st_ref, ob_buf_ref = refs[7:]
        tile = jnp.maximum(step - 1, 0)
        next_tile = jnp.minimum(step, pl.num_programs(0) - 2)
        slot = step % 2

        @pl.when(step == 0)
        def _():
            ob_buf_ref[...] = jnp.zeros_like(ob_buf_ref)

        @pl.when(next_tile % nb == 0)
        def _():
            st_ref[...] = jnp.zeros_like(st_ref)

        ob = ob_buf_ref[1 - slot]
    else:
        tile = step
        ob = ob_in_ref[...]
    in_lo = tile % nb < nb // 2
    oa_t = jnp.where(in_lo, oa_lo_ref[...], oa_hi_ref[...])
    merged, parts = {}, []

    def project(c):
        ya = lax.dot_general(oa_t, wpa_ref[:, c * gw:(c + 1) * gw], TN_DIMS, preferred_element_type=F32)
        yb = _dot(ob, wpb_ref[:, c * gw:(c + 1) * gw])
        merged[c] = (jax.nn.sigmoid(ga_refs[c][...].astype(F32)) * ya
                     + jax.nn.sigmoid(gb_refs[c][...].astype(F32)) * yb).astype(BF16)

    def output(c):
        parts.append(_dot(merged[c], wo_ref[c * gw:(c + 1) * gw, :]))

    def finish():
        o_ref[...] = _layer_norm(alpha * x_ref[...] + sum(parts[1:], parts[0]), g_ref[...], b_ref[...])

    def chain(*fs):
        return lambda: [f() for f in fs]

    steps = [functools.partial(project, 0)]
    steps += [chain(functools.partial(output, c - 1), functools.partial(project, c)) for c in range(1, n_gate)]
    pieces = steps + [functools.partial(output, n_gate - 1), finish]
    if hgrn:
        hp = hgrn["hp"]
        if hp % 2 == 0:
            half_w = hp // 2 * HEAD_DIM
            gens = [_hgrn_spans(*(r.at[:, pl.ds(g * half_w, half_w)] for r in (q_ref, f_ref, i_ref, og_ref)),
                                lb_ref.at[pl.ds(g * hp // 2, hp // 2)], gain_ref.at[pl.ds(g * hp // 2, hp // 2)],
                                ob_buf_ref.at[slot, :, pl.ds(g * half_w, half_w)],
                                st_ref.at[pl.ds(g * hp // 2, hp // 2)], n_spans=1, span=ob_buf_ref.shape[1],
                                as_generator=True, **dict(hgrn, hp=hp // 2)) for g in range(2)]
            first, second = gens
            pieces = iter(pieces)
            next(first)
            next(pieces, lambda: None)()
            next(first), next(second)
            next(pieces, lambda: None)()
            next(first), next(second)
            next(pieces, lambda: None)()
            for _ in first:
                pass
            next(second)
            for piece in pieces:
                piece()
            for _ in second:
                pass
            pieces = []
        else:
            n_hooks = 3 + ob_buf_ref.shape[1] // hgrn["chunk"]
            _hgrn_spans(q_ref, f_ref, i_ref, og_ref, lb_ref, gain_ref, ob_buf_ref.at[slot], st_ref, n_spans=1,
                        span=ob_buf_ref.shape[1], hooks=pieces[:n_hooks], **hgrn)
            pieces = pieces[n_hooks:]
    for piece in pieces:
        piece()


def _mixout(o_a_halves, o_b, z, x, wpa, wpb, wo, ln_g, ln_b, *, alpha, ga_off, gb_off, nb, hgrn=None):
    m, d = x.shape
    gw = math.gcd(math.gcd(ga_off, gb_off), d)
    n_gate = d // gw
    tm = MOBA_BLOCK
    half = nb // 2
    n_tiles = m // tm
    if hgrn:
        hb = hgrn["heads"]
        b_width = hb * HEAD_DIM
        assert tm % HGRN_CHUNK == 0 and all(hgrn[k] % hb == 0 for k in ("q_col", "f_col", "i_col", "og_col"))
        mix_tile = lambda i: jnp.maximum(i - 1, 0)
        next_tile = lambda i: jnp.minimum(i, n_tiles - 1)
        settings = dict(chunk=HGRN_CHUNK, sub=HGRN_SUB, hp=hb)
    else:
        mix_tile = lambda i: i
        settings = None
    kern = functools.partial(_mixout_kernel, alpha=alpha, n_gate=n_gate, gw=gw, nb=nb, hgrn=settings)

    def gate_spec(off, c):
        return pl.BlockSpec((tm, gw), lambda i: (mix_tile(i), off // gw + c))

    def whole(a):
        return pl.BlockSpec(a.shape, lambda i: (0,) * a.ndim, pipeline_mode=pl.Buffered(1))

    def rows(a):
        return pl.BlockSpec((tm, a.shape[1]), lambda i: (mix_tile(i), 0))

    def attn_spec(pos):
        def index(i):
            t = mix_tile(i)
            return 0, (t // nb) * half + pos(t % nb)
        return pl.BlockSpec((o_a_halves[0].shape[0], tm), index)

    attn_specs = [attn_spec(lambda blk: jnp.minimum(blk, half - 1)),
                  attn_spec(lambda blk: nb - 1 - jnp.maximum(blk, half))]
    tail_specs = ([gate_spec(ga_off, c) for c in range(n_gate)] + [gate_spec(gb_off, c) for c in range(n_gate)]
                  + [rows(x), whole(wpa), whole(wpb), whole(wo), whole(ln_g), whole(ln_b)])
    tail_args = (*([z] * (2 * n_gate)), x, wpa, wpb, wo, ln_g, ln_b)
    if hgrn:
        def col_spec(col):
            return pl.BlockSpec((tm, b_width), lambda i: (next_tile(i), col // hb))
        in_specs = ([col_spec(hgrn[k]) for k in ("q_col", "f_col", "i_col", "og_col")]
                    + [whole(hgrn["lb"]), whole(hgrn["gain"])] + attn_specs + tail_specs)
        args = (z, z, z, z, hgrn["lb"], hgrn["gain"], *o_a_halves, *tail_args)
        scratch = [pltpu.VMEM((hb, HEAD_DIM, HEAD_DIM), F32),
                   pltpu.VMEM((2, tm, b_width), BF16)]
    else:
        in_specs = attn_specs + [rows(o_b)] + tail_specs
        args = (*o_a_halves, o_b, *tail_args)
        scratch = []
    return pl.pallas_call(
        kern,
        grid=(n_tiles + (1 if hgrn else 0),),
        in_specs=in_specs,
        out_specs=pl.BlockSpec((tm, d), lambda i: (mix_tile(i), 0)),
        out_shape=jax.ShapeDtypeStruct((m, d), F32),
        scratch_shapes=scratch,
        compiler_params=_params("arbitrary"),
        name="hgrn_mixout" if hgrn else "mixout",
    )(*args)


def _ffn_kernel(h_ref, wg_ref, wu_ref, wd_ref, g_ref, b_ref, o_ref, hb_ref, *, alpha, nf):
    f = pl.program_id(1)
    chunk = min(FFN_ROW_CHUNK, hb_ref.shape[0])

    def step(first, last):
        for r in range(0, hb_ref.shape[0], chunk):
            rows = slice(r, r + chunk)
            if first:
                hb = h_ref[rows, :].astype(BF16)
                hb_ref[rows, :] = hb
            else:
                hb = hb_ref[rows, :]
            gate = _dot(hb, wg_ref[...])
            up = _dot(hb, wu_ref[...])
            act = (gate * _sigmoid(gate) * up).astype(BF16)
            down = _dot(act, wd_ref[...])
            acc = down if first else o_ref[rows, :] + down
            o_ref[rows, :] = _layer_norm(alpha * h_ref[rows, :] + acc, g_ref[...], b_ref[...]) if last else acc

    if nf == 1:
        step(True, True)
    else:
        pl.when(f == 0)(functools.partial(step, True, False))
        pl.when(f == nf - 1)(functools.partial(step, False, True))
        if nf > 2:
            pl.when((f > 0) & (f < nf - 1))(functools.partial(step, False, False))


def _ffn(h, wg, wu, wd, ln_g, ln_b, *, alpha):
    m, d = h.shape
    hidden = wg.shape[1]
    tm, tf = _tile(m, 1024), _tile(hidden, 512)
    assert tm % min(FFN_ROW_CHUNK, tm) == 0
    nf = hidden // tf
    kern = functools.partial(_ffn_kernel, alpha=alpha, nf=nf)
    vec = pl.BlockSpec((1, d), lambda i, f: (0, 0))
    return pl.pallas_call(
        kern,
        grid=(m // tm, nf),
        in_specs=[pl.BlockSpec((tm, d), lambda i, f: (i, 0)),
                  pl.BlockSpec((d, tf), lambda i, f: (0, f)),
                  pl.BlockSpec((d, tf), lambda i, f: (0, f)),
                  pl.BlockSpec((tf, d), lambda i, f: (f, 0)),
                  vec, vec],
        out_specs=pl.BlockSpec((tm, d), lambda i, f: (i, 0)),
        out_shape=jax.ShapeDtypeStruct((m, d), F32),
        scratch_shapes=[pltpu.VMEM((tm, d), BF16)],
        compiler_params=_params("parallel", "arbitrary"),
        name="ffn",
    )(h, wg, wu, wd, ln_g, ln_b)


def kernel(x, w_in, w_proj_a, w_proj_b, w_out, hgrn_norm_g, hgrn_lb_logits, ln1_g, ln1_b,
           w_gate_ffn, w_up_ffn, w_down_ffn, ln2_g, ln2_b):
    bsz, seq, d = x.shape
    depth = w_in.shape[0]
    a_width = w_proj_a.shape[1]
    b_vwidth = w_proj_b.shape[1]
    b_fwidth = hgrn_lb_logits.shape[1]
    assert a_width % HEAD_DIM == 0 and b_vwidth == b_fwidth and b_fwidth % HEAD_DIM == 0
    assert w_in.shape[2] == 3 * a_width + 2 * b_fwidth + 2 * b_vwidth + 2 * d
    a_heads = a_width // HEAD_DIM
    b_heads = b_fwidth // HEAD_DIM
    qa_col, ka_col, va_col = 0, a_heads, 2 * a_heads
    qb_col = 3 * a_heads
    fb_col, ib_col, ogb_col = qb_col + b_heads, qb_col + 2 * b_heads, qb_col + 3 * b_heads
    ga_off = 3 * a_width + 2 * b_fwidth + 2 * b_vwidth
    gb_off = ga_off + d
    alpha = (2.0 * depth) ** 0.25

    lb_all = jnp.cumsum(jax.nn.softmax(hgrn_lb_logits.astype(F32), axis=0), axis=0)

    h = x.reshape(bsz * seq, d)
    for l in range(depth):
        z = _inproj(h, w_in[l].astype(BF16))
        o_a = _moba(z, bsz, seq, a_heads, qa_col, ka_col, va_col)
        lb = lb_all[l].reshape(b_heads, 1, HEAD_DIM)
        gain = hgrn_norm_g[l].astype(F32).reshape(b_heads, 1, HEAD_DIM)
        b_cols = dict(heads=b_heads, q_col=qb_col, f_col=fb_col, i_col=ib_col, og_col=ogb_col)
        mix = functools.partial(_mixout, alpha=alpha, ga_off=ga_off, gb_off=gb_off, nb=seq // MOBA_BLOCK)
        mix_weights = (w_proj_a[l].astype(BF16), w_proj_b[l].astype(BF16), w_out[l].astype(BF16),
                       ln1_g[l].reshape(1, d), ln1_b[l].reshape(1, d))

        def fused(o_a, z, h, lb, gain, *w):
            return mix(o_a, None, z, h, *w, hgrn=dict(lb=lb, gain=gain, **b_cols))

        def split(o_a, z, h, lb, gain, *w):
            o_b = _hgrn_safe(z, lb, gain, bsz=bsz, seq=seq, **b_cols)
            return mix(o_a, o_b, z, h, *w)

        h = lax.cond(_hgrn_fast_ok(lb), fused, split, o_a, z, h, lb, gain, *mix_weights)
        h = _ffn(h, w_gate_ffn[l].astype(BF16), w_up_ffn[l].astype(BF16), w_down_ffn[l].astype(BF16),
                 ln2_g[l].reshape(1, d), ln2_b[l].reshape(1, d), alpha=alpha)
    return h.reshape(bsz, seq, d)
```
